```python
import math
import jax, jax.numpy as jnp
from jax import lax
import numpy as np

D_MODEL = 1024
BATCH = 2
SEQ = 8192
DEPTH = 4

N_MIXERS = 2
BLK = 128
ROPE_THETA = 500000.0
EPS = 1e-6
NEG_INF = -1e30

DILATED_GROUPS = ((128, 1), (512, 4), (2048, 16))
A_N_GROUPS = len(DILATED_GROUPS)
A_HEAD_DIM = 128
A_HEADS = D_MODEL // A_HEAD_DIM
A_WIDTH = A_HEADS * A_HEAD_DIM
A_ROT = A_HEAD_DIM // 4
A_QKV = 3 * A_N_GROUPS * A_WIDTH
A_IN = A_QKV + A_WIDTH

B_HEADS = 8
B_HEAD_DIM = 64
B_WIDTH = B_HEADS * 2 * B_HEAD_DIM
B_ROT = B_HEAD_DIM // 4
B_IN = 4 * B_WIDTH

N_A_LAYERS = (DEPTH + 1) // 2
N_B_LAYERS = DEPTH // 2

kernel_name = "hybrid_dilated_diff_gated_trunk"


def rms_norm(x, g):
    xf = x.astype(jnp.float32)
    y = xf * lax.rsqrt(jnp.mean(xf * xf, axis=-1, keepdims=True) + EPS)
    return (y * g.astype(jnp.float32)).astype(x.dtype)


def rope_tables(seq, rot_dim):
    inv = 1.0 / (ROPE_THETA ** (jnp.arange(0, rot_dim, 2, dtype=jnp.float32) / rot_dim))
    ang = jnp.arange(seq, dtype=jnp.float32)[:, None] * inv[None, :]
    return jnp.cos(ang), jnp.sin(ang)


def apply_partial_rope(x, cos, sin):
    half = cos.shape[-1]
    x1, x2, xp = x[..., :half], x[..., half:2 * half], x[..., 2 * half:]
    c = cos[None, :, None, :].astype(x.dtype)
    s = sin[None, :, None, :].astype(x.dtype)
    return jnp.concatenate([x1 * c - x2 * s, x2 * c + x1 * s, xp], axis=-1)


def shift_blocks(t, j):
    nb = t.shape[2]
    return jnp.pad(t, ((0, 0), (0, 0), (j, 0), (0, 0), (0, 0), (0, 0)))[:, :, :nb]


def dilated_group_attention(q, k, v, window, dilation):
    b, s, h, d = q.shape
    r = dilation
    w_sub = window // dilation
    n_prev = -(-w_sub // BLK)
    seg = r * BLK
    s_pad = -(-s // seg) * seg
    L = s_pad // r
    nb = L // BLK

    def to_blocks(t):
        t = jnp.pad(t, ((0, 0), (0, s_pad - s), (0, 0), (0, 0)))
        t = t.reshape(b, L, r, h, d).transpose(0, 2, 1, 3, 4)
        return t.reshape(b, r, nb, BLK, h, d)

    qb, kb, vb = to_blocks(q), to_blocks(k), to_blocks(v)
    kc = jnp.concatenate([shift_blocks(kb, j) for j in range(n_prev, 0, -1)] + [kb], axis=3)
    vc = jnp.concatenate([shift_blocks(vb, j) for j in range(n_prev, 0, -1)] + [vb], axis=3)

    scores = jnp.einsum('brnqhd,brnkhd->brnhqk', qb, kc,
                        preferred_element_type=jnp.float32) * (d ** -0.5)
    n_keys = (n_prev + 1) * BLK
    qi = jnp.arange(BLK)[:, None]
    ki = jnp.arange(n_keys)[None, :]
    dist = qi + n_prev * BLK - ki
    band = (dist >= 0) & (dist <= w_sub)
    kvalid = (jnp.arange(nb)[:, None] * BLK - n_prev * BLK + ki) >= 0
    mask = band[None, :, :] & kvalid[:, None, :]
    scores = jnp.where(mask[None, None, :, None, :, :], scores, NEG_INF)
    lse = jax.nn.logsumexp(scores, axis=-1)
    p = jnp.exp(scores - lse[..., None]).astype(v.dtype)
    o = jnp.einsum('brnhqk,brnkhd->brnqhd', p, vc)

    o = o.reshape(b, r, L, h, d).transpose(0, 2, 1, 3, 4).reshape(b, s_pad, h, d)[:, :s]
    lse = lse.transpose(0, 1, 2, 4, 3).reshape(b, r, L, h).transpose(0, 2, 1, 3)
    lse = lse.reshape(b, s_pad, h)[:, :s]
    return o, lse


def dilated_mixer(hn, w_in, w_out, cos, sin):
    b, s, _ = hn.shape
    proj = hn @ w_in
    qkv = proj[..., :A_QKV].reshape(b, s, A_N_GROUPS, 3, A_HEADS, A_HEAD_DIM)
    z = proj[..., A_QKV:]
    outs, lses = [], []
    for g, (window, dilation) in enumerate(DILATED_GROUPS):
        q = apply_partial_rope(qkv[:, :, g, 0], cos, sin)
        k = apply_partial_rope(qkv[:, :, g, 1], cos, sin)
        v = qkv[:, :, g, 2]
        o, l = dilated_group_attention(q, k, v, window, dilation)
        outs.append(o)
        lses.append(l)
    alpha = jax.nn.softmax(jnp.stack(lses, axis=0), axis=0)
    o = jnp.einsum('gbsh,gbshd->bshd', alpha.astype(outs[0].dtype), jnp.stack(outs, axis=0))
    y = o.reshape(b, s, A_WIDTH) * jax.nn.silu(z)
    return y @ w_out


def diff_mixer(hn, w_in, lam_params, subln, w_out, cos, sin, lam_init):
    b, s, _ = hn.shape
    proj = hn @ w_in
    q = proj[..., :B_WIDTH].reshape(b, s, B_HEADS, 2, B_HEAD_DIM)
    k = proj[..., B_WIDTH:2 * B_WIDTH].reshape(b, s, B_HEADS, 2, B_HEAD_DIM)
    v = proj[..., 2 * B_WIDTH:3 * B_WIDTH].reshape(b, s, B_HEADS, 2 * B_HEAD_DIM)
    z = proj[..., 3 * B_WIDTH:]
    q1 = apply_partial_rope(q[..., 0, :], cos, sin)
    q2 = apply_partial_rope(q[..., 1, :], cos, sin)
    k1 = apply_partial_rope(k[..., 0, :], cos, sin)
    k2 = apply_partial_rope(k[..., 1, :], cos, sin)

    lp = lam_params.astype(jnp.float32)
    lam = jnp.exp(jnp.sum(lp[0] * lp[1])) - jnp.exp(jnp.sum(lp[2] * lp[3])) + lam_init

    nb = s // BLK
    qb1 = q1.reshape(b, nb, BLK, B_HEADS, B_HEAD_DIM).transpose(1, 0, 2, 3, 4)
    qb2 = q2.reshape(b, nb, BLK, B_HEADS, B_HEAD_DIM).transpose(1, 0, 2, 3, 4)
    kpos = jnp.arange(s)
    scale = B_HEAD_DIM ** -0.5

    def block(args):
        i, a1, a2 = args
        qpos = i * BLK + jnp.arange(BLK)
        causal = (kpos[None, :] <= qpos[:, None])[None, None]
        s1 = jnp.einsum('bqhd,bkhd->bhqk', a1, k1, preferred_element_type=jnp.float32) * scale
        s2 = jnp.einsum('bqhd,bkhd->bhqk', a2, k2, preferred_element_type=jnp.float32) * scale
        p1 = jax.nn.softmax(jnp.where(causal, s1, NEG_INF), axis=-1)
        p2 = jax.nn.softmax(jnp.where(causal, s2, NEG_INF), axis=-1)
        a = (p1 - lam * p2).astype(v.dtype)
        return jnp.einsum('bhqk,bkhe->bqhe', a, v)

    o = lax.map(block, (jnp.arange(nb), qb1, qb2))
    o = o.transpose(1, 0, 2, 3, 4).reshape(b, s, B_HEADS, 2 * B_HEAD_DIM)
    o = rms_norm(o, subln) * (1.0 - lam_init)
    y = o.reshape(b, s, B_WIDTH) * jax.nn.silu(z)
    return y @ w_out


def setup_inputs(seed: int = 0) -> dict:
    key = jax.random.key(seed)
    ks = jax.random.split(key, 10)
    f32 = jnp.float32
    x = jax.random.normal(ks[0], (BATCH, SEQ, D_MODEL), f32)
    a_norm = 1.0 + 0.02 * jax.random.normal(ks[1], (N_A_LAYERS, D_MODEL), f32)
    a_w_in = jax.random.normal(ks[2], (N_A_LAYERS, D_MODEL, A_IN), f32) * D_MODEL ** -0.5
    a_w_out = jax.random.normal(ks[3], (N_A_LAYERS, A_WIDTH, D_MODEL), f32) * A_WIDTH ** -0.5
    b_norm = 1.0 + 0.02 * jax.random.normal(ks[4], (N_B_LAYERS, D_MODEL), f32)
    b_w_in = jax.random.normal(ks[5], (N_B_LAYERS, D_MODEL, B_IN), f32) * D_MODEL ** -0.5
    b_lambda = 0.1 * jax.random.normal(ks[6], (N_B_LAYERS, 4, B_HEAD_DIM), f32)
    b_subln = 1.0 + 0.02 * jax.random.normal(ks[7], (N_B_LAYERS, 2 * B_HEAD_DIM), f32)
    b_w_out = jax.random.normal(ks[8], (N_B_LAYERS, B_WIDTH, D_MODEL), f32) * B_WIDTH ** -0.5
    final_norm = 1.0 + 0.02 * jax.random.normal(ks[9], (D_MODEL,), f32)
    return {"x": x, "a_norm": a_norm, "a_w_in": a_w_in, "a_w_out": a_w_out,
            "b_norm": b_norm, "b_w_in": b_w_in, "b_lambda": b_lambda,
            "b_subln": b_subln, "b_w_out": b_w_out, "final_norm": final_norm}


def reference(x, a_norm, a_w_in, a_w_out, b_norm, b_w_in, b_lambda, b_subln, b_w_out, final_norm):
    s = x.shape[1]
    cos_a, sin_a = rope_tables(s, A_ROT)
    cos_b, sin_b = rope_tables(s, B_ROT)
    for i in range(DEPTH):
        j = i // N_MIXERS
        if i % N_MIXERS == 0:
            x = x + dilated_mixer(rms_norm(x, a_norm[j]), a_w_in[j], a_w_out[j], cos_a, sin_a)
        else:
            lam_init = 0.8 - 0.6 * math.exp(-0.3 * i)
            x = x + diff_mixer(rms_norm(x, b_norm[j]), b_w_in[j], b_lambda[j], b_subln[j],
                               b_w_out[j], cos_b, sin_b, lam_init)
    return rms_norm(x, final_norm)
```

```python
import functools
import math

import jax
import jax.numpy as jnp
from jax import lax
from jax.experimental import pallas as pl
from jax.experimental.pallas import tpu as pltpu

D_MODEL = 1024
DEPTH = 4
BLK = 128
ROPE_THETA = 500000.0
EPS = 1e-6
NEG = -1e30
LOG2E = 1.4426950408889634

DILATED_GROUPS = ((128, 1), (512, 4), (2048, 16))
A_HEAD_DIM = 128
A_HEADS = 8
A_ROT = A_HEAD_DIM // 4
B_HEADS = 8
B_HEAD_DIM = 64
B_ROT = B_HEAD_DIM // 4

LANES = 128
COL_TILE = 1024
ROW_TILE = 512
A_TQ = 512
B_TQ = 512
B_TK = 512
VMEM_LIMIT = 56 * 1024 * 1024

F32 = jnp.float32
BF16 = jnp.bfloat16


def _params(sem):
    return pltpu.CompilerParams(dimension_semantics=sem, vmem_limit_bytes=VMEM_LIMIT)


def _rope_tables(seq, head_dim, rot_dim):
    half = rot_dim // 2
    inv = 1.0 / (ROPE_THETA ** (jnp.arange(0, rot_dim, 2, dtype=F32) / rot_dim))
    ang = jnp.arange(seq, dtype=F32)[:, None] * inv[None, :]
    cos, sin = jnp.cos(ang), jnp.sin(ang)
    zeros = jnp.zeros((seq, head_dim - 2 * half), F32)
    zh = jnp.zeros((seq, half), F32)
    c = jnp.concatenate([cos, cos, jnp.ones_like(zeros)], axis=1)
    s1 = jnp.concatenate([zh, sin, zeros], axis=1)
    s2 = jnp.concatenate([-sin, zh, zeros], axis=1)
    reps = LANES // head_dim
    return tuple(jnp.tile(t, (1, reps)) for t in (c, s1, s2))


def _rms(x, g):
    ms = jnp.mean(x * x, axis=-1, keepdims=True)
    return x * lax.rsqrt(ms + EPS) * g


def _norm_kernel(x_ref, g_ref, o_ref):
    o_ref[...] = _rms(x_ref[...], g_ref[...]).astype(o_ref.dtype)


def _norm(x2d, g):
    rows = x2d.shape[0]
    return pl.pallas_call(
        _norm_kernel,
        grid=(rows // ROW_TILE,),
        in_specs=[pl.BlockSpec((ROW_TILE, D_MODEL), lambda i: (i, 0)),
                  pl.BlockSpec((1, D_MODEL), lambda i: (0, 0))],
        out_specs=pl.BlockSpec((ROW_TILE, D_MODEL), lambda i: (i, 0)),
        out_shape=jax.ShapeDtypeStruct((rows, D_MODEL), BF16),
        compiler_params=_params(("parallel",)),
        name="rmsnorm",
    )(x2d, g.reshape(1, D_MODEL))


def _proj_kernel(hn_ref, w_ref, c_ref, s1_ref, s2_ref, o_ref, *, kinds, shift, qscale):
    j = pl.program_id(3)
    acc = jnp.dot(hn_ref[...], w_ref[...], preferred_element_type=F32)

    def rope_store(scale):
        c, s1, s2 = c_ref[...], s1_ref[...], s2_ref[...]
        if scale != 1.0:
            c, s1, s2 = c * scale, s1 * scale, s2 * scale
        for h in range(COL_TILE // LANES):
            sl = slice(h * LANES, (h + 1) * LANES)
            a = acc[:, sl]
            r = a * c + pltpu.roll(a, shift, 1) * s1 + pltpu.roll(a, LANES - shift, 1) * s2
            o_ref[:, sl] = r.astype(o_ref.dtype)

    for idx, kind in enumerate(kinds):
        @pl.when(j == idx)
        def _(kind=kind):
            if kind == "q":
                rope_store(qscale)
            elif kind == "k":
                rope_store(1.0)
            elif kind == "v":
                o_ref[...] = acc.astype(o_ref.dtype)
            else:
                o_ref[...] = (acc * jax.nn.sigmoid(acc)).astype(o_ref.dtype)


def _proj(hn, w, tables, *, dilation, col_blocks, kinds, shift, qscale, name):
    bsz, seq, _ = hn.shape
    r = dilation
    length = seq // r
    tm = min(ROW_TILE, length)
    hn_v = hn.reshape(bsz, length, r * D_MODEL)
    tabs = [t.reshape(length, r * LANES) for t in tables]
    nk = len(kinds)
    cb = jnp.asarray(col_blocks, jnp.int32)

    def w_map(b, p, i, j, cb_ref):
        return (0, cb_ref[j])

    grid_spec = pltpu.PrefetchScalarGridSpec(
        num_scalar_prefetch=1,
        grid=(bsz, r, length // tm, nk),
        in_specs=[
            pl.BlockSpec((None, tm, D_MODEL), lambda b, p, i, j, cb_ref: (b, i, p)),
            pl.BlockSpec((D_MODEL, COL_TILE), w_map),
            pl.BlockSpec((tm, LANES), lambda b, p, i, j, cb_ref: (i, p)),
            pl.BlockSpec((tm, LANES), lambda b, p, i, j, cb_ref: (i, p)),
            pl.BlockSpec((tm, LANES), lambda b, p, i, j, cb_ref: (i, p)),
        ],
        out_specs=pl.BlockSpec((None, None, tm, COL_TILE),
                               lambda b, p, i, j, cb_ref: (b, p, i, j)),
    )
    kern = functools.partial(_proj_kernel_prefetch, kinds=kinds, shift=shift, qscale=qscale)
    return pl.pallas_call(
        kern,
        grid_spec=grid_spec,
        out_shape=jax.ShapeDtypeStruct((bsz, r, length, nk * COL_TILE), BF16),
        compiler_params=_params(("parallel", "parallel", "parallel", "arbitrary")),
        name=name,
    )(cb, hn_v, w, *tabs)


def _proj_kernel_prefetch(cb_ref, *refs, **kw):
    del cb_ref
    _proj_kernel(*refs, **kw)


def _dil_attn_kernel(q_ref, kc_ref, vc_ref, kp_ref, vp_ref, o_ref, lse_ref, kf, vf):
    n = pl.program_id(2)
    kf[0:BLK] = kp_ref[...]
    kf[BLK:] = kc_ref[...]
    vf[0:BLK] = vp_ref[...]
    vf[BLK:] = vc_ref[...]

    row = lax.broadcasted_iota(jnp.int32, (BLK, 2 * BLK), 0)
    col = lax.broadcasted_iota(jnp.int32, (BLK, 2 * BLK), 1)
    band = (col >= row) & (col <= row + BLK)
    band_first = band & ((col >= BLK) | (n > 0))
    lane = lax.broadcasted_iota(jnp.int32, (BLK, LANES), 1)

    for i in range(A_TQ // BLK):
        rows = slice(i * BLK, (i + 1) * BLK)
        keys = slice(i * BLK, (i + 2) * BLK)
        mask = band_first if i == 0 else band
        lse_tile = jnp.zeros((BLK, LANES), F32)
        for h in range(A_HEADS):
            sl = slice(h * A_HEAD_DIM, (h + 1) * A_HEAD_DIM)
            q = q_ref[rows, sl]
            k = kf[keys, sl]
            v = vf[keys, sl]
            s = lax.dot_general(q, k, (((1,), (1,)), ((), ())), preferred_element_type=F32)
            s = jnp.where(mask, s, NEG)
            m = jnp.max(s, axis=-1, keepdims=True)
            p = jnp.exp2(s - m)
            l = jnp.sum(p, axis=-1, keepdims=True)
            o = jnp.dot(p.astype(BF16), v, preferred_element_type=F32)
            o_ref[rows, sl] = (o / l).astype(o_ref.dtype)
            lse_tile = jnp.where(lane == h, m + jnp.log2(l), lse_tile)
        lse_ref[rows, :] = lse_tile


def _dil_attn(qkv, *, name):
    bsz, r, length, _ = qkv.shape
    width = A_HEADS * A_HEAD_DIM
    sub = A_TQ // BLK
    blk_big = lambda c: pl.BlockSpec((None, None, A_TQ, width), lambda b, p, n: (b, p, n, c))
    blk_prev = lambda c: pl.BlockSpec(
        (None, None, BLK, width), lambda b, p, n: (b, p, jnp.maximum(n * sub - 1, 0), c))
    o, lse = pl.pallas_call(
        _dil_attn_kernel,
        grid=(bsz, r, length // A_TQ),
        in_specs=[blk_big(0), blk_big(1), blk_big(2), blk_prev(1), blk_prev(2)],
        out_specs=[pl.BlockSpec((None, A_TQ, width), lambda b, p, n: (b, n, p)),
                   pl.BlockSpec((None, A_TQ, LANES), lambda b, p, n: (b, n, p))],
        out_shape=[jax.ShapeDtypeStruct((bsz, length, r * width), BF16),
                   jax.ShapeDtypeStruct((bsz, length, r * LANES), F32)],
        scratch_shapes=[pltpu.VMEM((A_TQ + BLK, width), BF16),
                        pltpu.VMEM((A_TQ + BLK, width), BF16)],
        compiler_params=_params(("parallel", "parallel", "arbitrary")),
        name=name,
    )(qkv, qkv, qkv, qkv, qkv)
    seq = length * r
    return o.reshape(bsz * seq, width), lse.reshape(bsz * seq, LANES)


def _diff_attn_kernel(q_ref, k_ref, v_ref, lp_ref, g_ref, o_ref, *, lam_init):
    qi = pl.program_id(2)
    q = q_ref[...]
    lane = lax.broadcasted_iota(jnp.int32, q.shape, 1)
    zero = jnp.zeros_like(q)
    qq = jnp.concatenate([jnp.where(lane < B_HEAD_DIM, q, zero),
                          jnp.where(lane >= B_HEAD_DIM, q, zero)], axis=0)
    rows2 = 2 * B_TQ

    def step(kb, carry, masked):
        m, l, acc = carry
        start = pl.multiple_of(kb * B_TK, B_TK)
        k = k_ref[pl.ds(start, B_TK), :]
        v = v_ref[pl.ds(start, B_TK), :]
        s = lax.dot_general(qq, k, (((1,), (1,)), ((), ())), preferred_element_type=F32)
        if masked:
            r_i = lax.broadcasted_iota(jnp.int32, (rows2, B_TK), 0) & (B_TQ - 1)
            c_i = lax.broadcasted_iota(jnp.int32, (rows2, B_TK), 1)
            s = jnp.where(c_i <= r_i, s, NEG)
        m_new = jnp.maximum(m, jnp.max(s, axis=-1, keepdims=True))
        alpha = jnp.exp2(m - m_new)
        p = jnp.exp2(s - m_new)
        l = alpha * l + jnp.sum(p, axis=-1, keepdims=True)
        acc = alpha * acc + jnp.dot(p.astype(BF16), v, preferred_element_type=F32)
        return m_new, l, acc

    init = (jnp.full((rows2, 1), NEG, F32), jnp.zeros((rows2, 1), F32),
            jnp.zeros((rows2, 2 * B_HEAD_DIM), F32))
    carry = lax.fori_loop(0, qi, lambda kb, c: step(kb, c, False), init)
    m, l, acc = step(qi, carry, True)

    lp = lp_ref[...]
    lam = (jnp.exp(jnp.sum(lp[0:1] * lp[1:2], axis=-1, keepdims=True))
           - jnp.exp(jnp.sum(lp[2:3] * lp[3:4], axis=-1, keepdims=True)) + lam_init)
    on = acc / l
    o = on[:B_TQ] - lam * on[B_TQ:]
    o = _rms(o, g_ref[...]) * (1.0 - lam_init)
    o_ref[...] = o.astype(o_ref.dtype)


def _diff_attn(qkvz, lam_params, subln, lam_init, *, name):
    bsz, seq, _ = qkvz.shape
    hd = 2 * B_HEAD_DIM
    assert B_TQ == B_TK
    o = pl.pallas_call(
        functools.partial(_diff_attn_kernel, lam_init=lam_init),
        grid=(bsz, B_HEADS, seq // B_TQ),
        in_specs=[pl.BlockSpec((None, B_TQ, hd), lambda b, h, i: (b, i, h)),
                  pl.BlockSpec((None, seq, hd), lambda b, h, i: (b, 0, B_HEADS + h)),
                  pl.BlockSpec((None, seq, hd), lambda b, h, i: (b, 0, 2 * B_HEADS + h)),
                  pl.BlockSpec((4, B_HEAD_DIM), lambda b, h, i: (0, 0)),
                  pl.BlockSpec((1, hd), lambda b, h, i: (0, 0))],
        out_specs=pl.BlockSpec((None, B_TQ, hd), lambda b, h, i: (b, i, h)),
        out_shape=jax.ShapeDtypeStruct((bsz, seq, B_HEADS * hd), BF16),
        compiler_params=_params(("parallel", "parallel", "arbitrary")),
        name=name,
    )(qkvz, qkvz, qkvz, lam_params, subln.reshape(1, hd))
    return o.reshape(bsz * seq, B_HEADS * hd)


def _finish(y, x_ref, w_ref, g_ref, xo_ref, hn_ref, final):
    xn = x_ref[...] + jnp.dot(y, w_ref[...], preferred_element_type=F32)
    if final:
        xo_ref[...] = _rms(xn, g_ref[...])
    else:
        xo_ref[...] = xn
        hn_ref[...] = _rms(xn, g_ref[...]).astype(hn_ref.dtype)


def _a_out_kernel(o1, o2, o3, l1, l2, l3, z_ref, x_ref, w_ref, g_ref, xo_ref, hn_ref):
    a1, a2, a3 = l1[...], l2[...], l3[...]
    m = jnp.maximum(jnp.maximum(a1, a2), a3)
    e1, e2, e3 = jnp.exp2(a1 - m), jnp.exp2(a2 - m), jnp.exp2(a3 - m)
    inv = 1.0 / (e1 + e2 + e3)
    e1, e2, e3 = e1 * inv, e2 * inv, e3 * inv
    pieces = []
    for h in range(A_HEADS):
        sl = slice(h * A_HEAD_DIM, (h + 1) * A_HEAD_DIM)
        o = (e1[:, h:h + 1] * o1[:, sl].astype(F32)
             + e2[:, h:h + 1] * o2[:, sl].astype(F32)
             + e3[:, h:h + 1] * o3[:, sl].astype(F32))
        pieces.append((o * z_ref[:, sl].astype(F32)).astype(BF16))
    y = jnp.concatenate(pieces, axis=1)
    _finish(y, x_ref, w_ref, g_ref, xo_ref, hn_ref, False)


def _b_out_kernel(o_ref, z_ref, x_ref, w_ref, g_ref, xo_ref, *maybe_hn, final):
    y = (o_ref[...].astype(F32) * z_ref[...].astype(F32)).astype(BF16)
    _finish(y, x_ref, w_ref, g_ref, xo_ref, None if final else maybe_hn[0], final)


def _row_spec(width, col=0):
    return pl.BlockSpec((ROW_TILE, width), lambda i: (i, col))


def _out_call(kern, acts, act_specs, x2d, w, g, *, final, name):
    rows = x2d.shape[0]
    out_shape = [jax.ShapeDtypeStruct((rows, D_MODEL), F32)]
    out_specs = [_row_spec(D_MODEL)]
    if not final:
        out_shape.append(jax.ShapeDtypeStruct((rows, D_MODEL), BF16))
        out_specs.append(_row_spec(D_MODEL))
    res = pl.pallas_call(
        kern,
        grid=(rows // ROW_TILE,),
        in_specs=act_specs + [_row_spec(D_MODEL),
                              pl.BlockSpec((D_MODEL, D_MODEL), lambda i: (0, 0)),
                              pl.BlockSpec((1, D_MODEL), lambda i: (0, 0))],
        out_specs=out_specs,
        out_shape=out_shape,
        compiler_params=_params(("parallel",)),
        name=name,
    )(*acts, x2d, w, g.reshape(1, D_MODEL))
    return res if not final else (res[0], None)


def _dilated_layer(x2d, hn, w_in, w_out, g_next, tables, bsz, seq, tag):
    outs, lses = [], []
    z2d = None
    qscale = (A_HEAD_DIM ** -0.5) * LOG2E
    for g, (window, r) in enumerate(DILATED_GROUPS):
        assert window // r == BLK
        with_gate = g == 0
        kinds = ("q", "k", "v", "z") if with_gate else ("q", "k", "v")
        cols = [3 * g, 3 * g + 1, 3 * g + 2] + ([9] if with_gate else [])
        qkv = _proj(hn.reshape(bsz, seq, D_MODEL), w_in, tables, dilation=r, col_blocks=cols,
                    kinds=kinds, shift=A_ROT // 2, qscale=qscale, name=f"a_proj{g}_{tag}")
        if with_gate:
            z2d = qkv.reshape(bsz * seq, 4 * COL_TILE)
        o, lse = _dil_attn(qkv, name=f"a_attn{g}_{tag}")
        outs.append(o)
        lses.append(lse)
    acts = outs + lses + [z2d]
    specs = ([_row_spec(D_MODEL)] * 3 + [_row_spec(LANES)] * 3 + [_row_spec(COL_TILE, 3)])
    return _out_call(_a_out_kernel, acts, specs, x2d, w_out, g_next, final=False,
                     name=f"a_out_{tag}")


def _diff_layer(x2d, hn, w_in, lam_params, subln, w_out, g_next, tables, lam_init,
                bsz, seq, final, tag):
    qscale = (B_HEAD_DIM ** -0.5) * LOG2E
    qkvz = _proj(hn.reshape(bsz, seq, D_MODEL), w_in, tables, dilation=1,
                 col_blocks=[0, 1, 2, 3], kinds=("q", "k", "v", "z"), shift=B_ROT // 2,
                 qscale=qscale, name=f"b_proj_{tag}")
    qkvz = qkvz.reshape(bsz, seq, 4 * COL_TILE)
    o = _diff_attn(qkvz, lam_params, subln, lam_init, name=f"b_attn_{tag}")
    acts = [o, qkvz.reshape(bsz * seq, 4 * COL_TILE)]
    specs = [_row_spec(D_MODEL), _row_spec(COL_TILE, 3)]
    return _out_call(functools.partial(_b_out_kernel, final=final), acts, specs, x2d, w_out,
                     g_next, final=final, name=f"b_out_{tag}")


def kernel(x, a_norm, a_w_in, a_w_out, b_norm, b_w_in, b_lambda, b_subln, b_w_out, final_norm):
    bsz, seq, _ = x.shape
    tab_a = _rope_tables(seq, A_HEAD_DIM, A_ROT)
    tab_b = _rope_tables(seq, B_HEAD_DIM, B_ROT)
    a_w_in, a_w_out = a_w_in.astype(BF16), a_w_out.astype(BF16)
    b_w_in, b_w_out = b_w_in.astype(BF16), b_w_out.astype(BF16)

    x2d = x.reshape(bsz * seq, D_MODEL)
    hn = _norm(x2d, a_norm[0])
    for i in range(DEPTH):
        j = i // 2
        if i % 2 == 0:
            x2d, hn = _dilated_layer(x2d, hn, a_w_in[j], a_w_out[j], b_norm[j], tab_a,
                                     bsz, seq, tag=str(j))
        else:
            final = i == DEPTH - 1
            g_next = final_norm if final else a_norm[j + 1]
            lam_init = 0.8 - 0.6 * math.exp(-0.3 * i)
            x2d, hn = _diff_layer(x2d, hn, b_w_in[j], b_lambda[j], b_subln[j], b_w_out[j],
                                  g_next, tab_b, lam_init, bsz, seq, final, tag=str(j))
    return x2d.reshape(bsz, seq, D_MODEL)
```

```python
import functools
import math

import jax
import jax.numpy as jnp
from jax import lax
from jax.experimental import pallas as pl
from jax.experimental.pallas import tpu as pltpu

D_MODEL = 1024
DEPTH = 4
BLK = 128
ROPE_THETA = 500000.0
EPS = 1e-6
NEG = -1e30
LOG2E = 1.4426950408889634

DILATED_GROUPS = ((128, 1), (512, 4), (2048, 16))
A_HEAD_DIM = 128
A_HEADS = 8
A_ROT = A_HEAD_DIM // 4
B_HEADS = 8
B_HEAD_DIM = 64
B_ROT = B_HEAD_DIM // 4

LANES = 128
COL_TILE = 1024
ROW_TILE = 512
A_TQ = 512
B_TQ = 512
B_TK = 512
VMEM_LIMIT = 56 * 1024 * 1024

F32 = jnp.float32
BF16 = jnp.bfloat16


def _params(sem):
    return pltpu.CompilerParams(dimension_semantics=sem, vmem_limit_bytes=VMEM_LIMIT)


def _rope_tables(seq, head_dim, rot_dim):
    half = rot_dim // 2
    inv = 1.0 / (ROPE_THETA ** (jnp.arange(0, rot_dim, 2, dtype=F32) / rot_dim))
    ang = jnp.arange(seq, dtype=F32)[:, None] * inv[None, :]
    cos, sin = jnp.cos(ang), jnp.sin(ang)
    zeros = jnp.zeros((seq, head_dim - 2 * half), F32)
    zh = jnp.zeros((seq, half), F32)
    c = jnp.concatenate([cos, cos, jnp.ones_like(zeros)], axis=1)
    s1 = jnp.concatenate([zh, sin, zeros], axis=1)
    s2 = jnp.concatenate([-sin, zh, zeros], axis=1)
    reps = LANES // head_dim
    return tuple(jnp.tile(t, (1, reps)) for t in (c, s1, s2))


def _rms(x, g):
    ms = jnp.mean(x * x, axis=-1, keepdims=True)
    return x * lax.rsqrt(ms + EPS) * g


def _norm_kernel(x_ref, g_ref, o_ref):
    o_ref[...] = _rms(x_ref[...], g_ref[...]).astype(o_ref.dtype)


def _norm(x2d, g):
    rows = x2d.shape[0]
    return pl.pallas_call(
        _norm_kernel,
        grid=(rows // ROW_TILE,),
        in_specs=[pl.BlockSpec((ROW_TILE, D_MODEL), lambda i: (i, 0)),
                  pl.BlockSpec((1, D_MODEL), lambda i: (0, 0))],
        out_specs=pl.BlockSpec((ROW_TILE, D_MODEL), lambda i: (i, 0)),
        out_shape=jax.ShapeDtypeStruct((rows, D_MODEL), BF16),
        compiler_params=_params(("parallel",)),
        name="rmsnorm",
    )(x2d, g.reshape(1, D_MODEL))


def _strided_rows(ref, lead, p, rows, r):
    idx = pl.ds(p, rows, stride=r) if r > 1 else slice(None)
    return ref[lead + (idx, slice(None))]


def _proj_kernel(cb_ref, hn_ref, w_ref, c_ref, s1_ref, s2_ref, o_ref, acc0, acc1, *,
                 nk, r, has_gate, shift, qscale):
    del cb_ref
    t = pl.program_id(0)
    tm = hn_ref.shape[0]
    rows = tm // r
    nslab = COL_TILE // LANES

    @pl.when(t == 0)
    def _():
        acc1[...] = jnp.zeros(acc1.shape, F32)

    j = jnp.maximum(t - 1, 0) % nk
    is_rope = j < 2
    scale = jnp.where(j == 0, qscale, 1.0).astype(F32)

    def run(acc_new, acc_old):
        acc = jnp.dot(hn_ref[...], w_ref[...], preferred_element_type=F32)
        for h in range(nslab):
            acc_new[h] = acc[:, h * LANES:(h + 1) * LANES]
        for p in range(r):
            c = jnp.where(is_rope, _strided_rows(c_ref, (), p, rows, r) * scale, 1.0)
            s1 = jnp.where(is_rope, _strided_rows(s1_ref, (), p, rows, r) * scale, 0.0)
            s2 = jnp.where(is_rope, _strided_rows(s2_ref, (), p, rows, r) * scale, 0.0)
            for h in range(nslab):
                a = _strided_rows(acc_old, (h,), p, rows, r)
                v = a * c + pltpu.roll(a, shift, 1) * s1 + pltpu.roll(a, LANES - shift, 1) * s2
                if has_gate:
                    v = jnp.where(j == 3, v * jax.nn.sigmoid(v), v)
                o_ref[p, :, h * LANES:(h + 1) * LANES] = v.astype(o_ref.dtype)

    @pl.when(t % 2 == 0)
    def _():
        run(acc0, acc1)

    @pl.when(t % 2 == 1)
    def _():
        run(acc1, acc0)


def _proj(hn, w, tables, *, dilation, col_blocks, kinds, shift, qscale, name):
    assert kinds[:3] == ("q", "k", "v") and kinds[3:] in ((), ("z",))
    bsz, seq, _ = hn.shape
    r = dilation
    tm = ROW_TILE
    ni = seq // tm
    nk = len(kinds)
    n_tiles = bsz * ni * nk
    cb = jnp.asarray(col_blocks, jnp.int32)

    def tile(t):
        return t // (nk * ni), (t // nk) % ni, t % nk

    def cur(t):
        return tile(jnp.minimum(t, n_tiles - 1))

    def prev(t):
        return tile(jnp.maximum(t - 1, 0))

    tab_spec = pl.BlockSpec((tm, LANES), lambda t, cb_ref: (prev(t)[1], 0))
    grid_spec = pltpu.PrefetchScalarGridSpec(
        num_scalar_prefetch=1,
        grid=(n_tiles + 1,),
        in_specs=[
            pl.BlockSpec((None, tm, D_MODEL), lambda t, cb_ref: (cur(t)[0], cur(t)[1], 0)),
            pl.BlockSpec((D_MODEL, COL_TILE), lambda t, cb_ref: (0, cb_ref[cur(t)[2]])),
            tab_spec, tab_spec, tab_spec,
        ],
        out_specs=pl.BlockSpec((None, r, tm // r, COL_TILE),
                               lambda t, cb_ref: (prev(t)[0], 0, prev(t)[1], prev(t)[2])),
        scratch_shapes=[pltpu.VMEM((COL_TILE // LANES, tm, LANES), F32),
                        pltpu.VMEM((COL_TILE // LANES, tm, LANES), F32)],
    )
    kern = functools.partial(_proj_kernel, nk=nk, r=r, has_gate="z" in kinds, shift=shift,
                             qscale=qscale)
    return pl.pallas_call(
        kern,
        grid_spec=grid_spec,
        out_shape=jax.ShapeDtypeStruct((bsz, r, seq // r, nk * COL_TILE), BF16),
        compiler_params=_params(("arbitrary",)),
        name=name,
    )(cb, hn, w, *tables)


def _dil_attn_kernel(q_ref, kc_ref, vc_ref, kp_ref, vp_ref, o_ref, lse_ref, kf, vf):
    n = pl.program_id(2)
    kf[0:BLK] = kp_ref[...]
    kf[BLK:] = kc_ref[...]
    vf[0:BLK] = vp_ref[...]
    vf[BLK:] = vc_ref[...]

    row = lax.broadcasted_iota(jnp.int32, (BLK, 2 * BLK), 0)
    col = lax.broadcasted_iota(jnp.int32, (BLK, 2 * BLK), 1)
    band = (col >= row) & (col <= row + BLK)
    band_first = band & ((col >= BLK) | (n > 0))
    lane = lax.broadcasted_iota(jnp.int32, (BLK, LANES), 1)

    for i in range(A_TQ // BLK):
        rows = slice(i * BLK, (i + 1) * BLK)
        keys = slice(i * BLK, (i + 2) * BLK)
        mask = band_first if i == 0 else band
        lse_tile = jnp.zeros((BLK, LANES), F32)
        for h in range(A_HEADS):
            sl = slice(h * A_HEAD_DIM, (h + 1) * A_HEAD_DIM)
            q = q_ref[rows, sl]
            k = kf[keys, sl]
            v = vf[keys, sl]
            s = lax.dot_general(q, k, (((1,), (1,)), ((), ())), preferred_element_type=F32)
            s = jnp.where(mask, s, NEG)
            m = jnp.max(s, axis=-1, keepdims=True)
            p = jnp.exp2(s - m)
            l = jnp.sum(p, axis=-1, keepdims=True)
            o = jnp.dot(p.astype(BF16), v, preferred_element_type=F32)
            o_ref[rows, sl] = (o / l).astype(o_ref.dtype)
            lse_tile = jnp.where(lane == h, m + jnp.log2(l), lse_tile)
        lse_ref[rows, :] = lse_tile


def _dil_attn(qkv, *, name):
    bsz, r, length, _ = qkv.shape
    width = A_HEADS * A_HEAD_DIM
    sub = A_TQ // BLK
    blk_big = lambda c: pl.BlockSpec((None, None, A_TQ, width), lambda b, p, n: (b, p, n, c))
    blk_prev = lambda c: pl.BlockSpec(
        (None, None, BLK, width), lambda b, p, n: (b, p, jnp.maximum(n * sub - 1, 0), c))
    return pl.pallas_call(
        _dil_attn_kernel,
        grid=(bsz, r, length // A_TQ),
        in_specs=[blk_big(0), blk_big(1), blk_big(2), blk_prev(1), blk_prev(2)],
        out_specs=[pl.BlockSpec((None, None, A_TQ, width), lambda b, p, n: (b, p, n, 0)),
                   pl.BlockSpec((None, None, A_TQ, LANES), lambda b, p, n: (b, p, n, 0))],
        out_shape=[jax.ShapeDtypeStruct((bsz, r, length, width), BF16),
                   jax.ShapeDtypeStruct((bsz, r, length, LANES), F32)],
        scratch_shapes=[pltpu.VMEM((A_TQ + BLK, width), BF16),
                        pltpu.VMEM((A_TQ + BLK, width), BF16)],
        compiler_params=_params(("parallel", "parallel", "arbitrary")),
        name=name,
    )(qkv, qkv, qkv, qkv, qkv)


def _diff_attn_kernel(q_ref, k_ref, v_ref, lp_ref, g_ref, o_ref, qq, vx, m_s, acc_s, s0, s1, *,
                      lam_init):
    qi = pl.program_id(2)
    hd = 2 * B_HEAD_DIM

    @pl.when(qi == 0)
    def _():
        vx[:, :hd] = v_ref[...]
        vx[:, hd:] = jnp.ones((vx.shape[0], hd), vx.dtype)

    q = q_ref[...]
    lane = lax.broadcasted_iota(jnp.int32, q.shape, 1)
    zero = jnp.zeros_like(q)
    qq[:B_TQ] = jnp.where(lane < B_HEAD_DIM, q, zero)
    qq[B_TQ:] = jnp.where(lane >= B_HEAD_DIM, q, zero)
    m_s[...] = jnp.full(m_s.shape, NEG, F32)
    acc_s[...] = jnp.zeros(acc_s.shape, F32)
    rows2 = 2 * B_TQ

    def scores(kb, buf):
        start = pl.multiple_of(kb * B_TK, B_TK)
        k = k_ref[pl.ds(start, B_TK), :]
        buf[...] = lax.dot_general(qq[...], k, (((1,), (1,)), ((), ())),
                                   preferred_element_type=F32)

    def consume(kb, buf, masked):
        start = pl.multiple_of(kb * B_TK, B_TK)
        v = vx[pl.ds(start, B_TK), :]
        s = buf[...]
        if masked:
            r_i = lax.broadcasted_iota(jnp.int32, (rows2, B_TK), 0) & (B_TQ - 1)
            c_i = lax.broadcasted_iota(jnp.int32, (rows2, B_TK), 1)
            s = jnp.where(c_i <= r_i, s, NEG)
        m_prev = m_s[...]
        m_new = jnp.maximum(m_prev, jnp.max(s, axis=-1, keepdims=True))
        alpha = jnp.exp2(m_prev - m_new)
        p = jnp.exp2(s - jnp.tile(m_new, (1, B_TK // LANES)))
        pv = jnp.dot(p.astype(BF16), v, preferred_element_type=F32)
        acc_s[...] = jnp.tile(alpha, (1, 2)) * acc_s[...] + pv
        m_s[...] = m_new

    scores(0, s0)

    def pair(j, c):
        scores(2 * j + 1, s1)
        consume(2 * j, s0, False)
        scores(2 * j + 2, s0)
        consume(2 * j + 1, s1, False)
        return c

    lax.fori_loop(0, qi // 2, pair, 0)

    @pl.when(qi % 2 == 0)
    def _():
        consume(qi, s0, True)

    @pl.when(qi % 2 == 1)
    def _():
        scores(qi, s1)
        consume(qi - 1, s0, False)
        consume(qi, s1, True)

    lp = lp_ref[...]
    lam = (jnp.exp(jnp.sum(lp[0:1] * lp[1:2], axis=-1, keepdims=True))
           - jnp.exp(jnp.sum(lp[2:3] * lp[3:4], axis=-1, keepdims=True)) + lam_init)
    on = acc_s[:, :hd] / acc_s[:, hd:]
    o = on[:B_TQ] - lam * on[B_TQ:]
    o = _rms(o, g_ref[...]) * (1.0 - lam_init)
    o_ref[...] = o.astype(o_ref.dtype)


def _diff_attn(qkvz, lam_params, subln, lam_init, *, name):
    bsz, seq, _ = qkvz.shape
    hd = 2 * B_HEAD_DIM
    assert B_TQ == B_TK
    o = pl.pallas_call(
        functools.partial(_diff_attn_kernel, lam_init=lam_init),
        grid=(bsz, B_HEADS, seq // B_TQ),
        in_specs=[pl.BlockSpec((None, B_TQ, hd), lambda b, h, i: (b, i, h)),
                  pl.BlockSpec((None, seq, hd), lambda b, h, i: (b, 0, B_HEADS + h)),
                  pl.BlockSpec((None, seq, hd), lambda b, h, i: (b, 0, 2 * B_HEADS + h)),
                  pl.BlockSpec((4, B_HEAD_DIM), lambda b, h, i: (0, 0)),
                  pl.BlockSpec((1, hd), lambda b, h, i: (0, 0))],
        out_specs=pl.BlockSpec((None, B_TQ, hd), lambda b, h, i: (b, i, h)),
        out_shape=jax.ShapeDtypeStruct((bsz, seq, B_HEADS * hd), BF16),
        scratch_shapes=[pltpu.VMEM((2 * B_TQ, hd), BF16),
                        pltpu.VMEM((seq, 2 * hd), BF16),
                        pltpu.VMEM((2 * B_TQ, LANES), F32),
                        pltpu.VMEM((2 * B_TQ, 2 * hd), F32),
                        pltpu.VMEM((2 * B_TQ, B_TK), F32),
                        pltpu.VMEM((2 * B_TQ, B_TK), F32)],
        compiler_params=_params(("arbitrary", "arbitrary", "arbitrary")),
        name=name,
    )(qkvz, qkvz, qkvz, lam_params, subln.reshape(1, hd))
    return o.reshape(bsz * seq, B_HEADS * hd)


def _finish(y, x_ref, w_ref, g_ref, xo_ref, hn_ref, final):
    xn = x_ref[...] + jnp.dot(y, w_ref[...], preferred_element_type=F32)
    if final:
        xo_ref[...] = _rms(xn, g_ref[...])
    else:
        xo_ref[...] = xn
        hn_ref[...] = _rms(xn, g_ref[...]).astype(hn_ref.dtype)


def _a_out_kernel(o1, o2, o3, l1, l2, l3, z_ref, x_ref, w_ref, g_ref, xo_ref, hn_ref,
                  t2, t3, lt2, lt3):
    for o_ref, l_ref, tok, lt in ((o2, l2, t2, lt2), (o3, l3, t3, lt3)):
        r, rows = o_ref.shape[0], o_ref.shape[1]
        for p in range(r):
            idx = pl.ds(p, rows, stride=r)
            lt[idx, :] = l_ref[p]
            for h in range(A_HEADS):
                tok[h, idx, :] = o_ref[p, :, h * LANES:(h + 1) * LANES].astype(F32)
    a1, a2, a3 = l1[0], lt2[...], lt3[...]
    m = jnp.maximum(jnp.maximum(a1, a2), a3)
    e1, e2, e3 = jnp.exp2(a1 - m), jnp.exp2(a2 - m), jnp.exp2(a3 - m)
    inv = 1.0 / (e1 + e2 + e3)
    width = A_HEADS * A_HEAD_DIM
    spread = (lax.broadcasted_iota(jnp.int32, (LANES, width), 0)
              == lax.broadcasted_iota(jnp.int32, (LANES, width), 1) // A_HEAD_DIM).astype(BF16)
    w1, w2, w3 = (jnp.dot((e * inv).astype(BF16), spread, preferred_element_type=F32)
                  for e in (e1, e2, e3))
    pieces = []
    for h in range(A_HEADS):
        sl = slice(h * A_HEAD_DIM, (h + 1) * A_HEAD_DIM)
        o = w1[:, sl] * o1[0, :, sl].astype(F32) + w2[:, sl] * t2[h] + w3[:, sl] * t3[h]
        pieces.append((o * z_ref[:, sl].astype(F32)).astype(BF16))
    y = jnp.concatenate(pieces, axis=1)
    _finish(y, x_ref, w_ref, g_ref, xo_ref, hn_ref, False)


def _b_out_kernel(o_ref, z_ref, x_ref, w_ref, g_ref, xo_ref, *maybe_hn, final):
    y = (o_ref[...].astype(F32) * z_ref[...].astype(F32)).astype(BF16)
    _finish(y, x_ref, w_ref, g_ref, xo_ref, None if final else maybe_hn[0], final)


def _row_spec(width, col=0):
    return pl.BlockSpec((ROW_TILE, width), lambda i: (i, col))


def _dilated_spec(arr, seq):
    _, r, _, width = arr.shape
    ni = seq // ROW_TILE
    return pl.BlockSpec((None, r, ROW_TILE // r, width), lambda i: (i // ni, 0, i % ni, 0))


def _out_call(kern, acts, act_specs, x2d, w, g, *, final, name, scratch=()):
    rows = x2d.shape[0]
    out_shape = [jax.ShapeDtypeStruct((rows, D_MODEL), F32)]
    out_specs = [_row_spec(D_MODEL)]
    if not final:
        out_shape.append(jax.ShapeDtypeStruct((rows, D_MODEL), BF16))
        out_specs.append(_row_spec(D_MODEL))
    res = pl.pallas_call(
        kern,
        grid=(rows // ROW_TILE,),
        in_specs=act_specs + [_row_spec(D_MODEL),
                              pl.BlockSpec((D_MODEL, D_MODEL), lambda i: (0, 0)),
                              pl.BlockSpec((1, D_MODEL), lambda i: (0, 0))],
        out_specs=out_specs,
        out_shape=out_shape,
        scratch_shapes=list(scratch),
        compiler_params=_params(("parallel",)),
        name=name,
    )(*acts, x2d, w, g.reshape(1, D_MODEL))
    return res if not final else (res[0], None)


def _dilated_layer(x2d, hn, w_in, w_out, g_next, tables, bsz, seq, tag):
    outs, lses = [], []
    z2d = None
    qscale = (A_HEAD_DIM ** -0.5) * LOG2E
    for g, (window, r) in enumerate(DILATED_GROUPS):
        assert window // r == BLK
        with_gate = g == 0
        kinds = ("q", "k", "v", "z") if with_gate else ("q", "k", "v")
        cols = [3 * g, 3 * g + 1, 3 * g + 2] + ([9] if with_gate else [])
        qkv = _proj(hn.reshape(bsz, seq, D_MODEL), w_in, tables, dilation=r, col_blocks=cols,
                    kinds=kinds, shift=A_ROT // 2, qscale=qscale, name=f"a_proj{g}_{tag}")
        if with_gate:
            z2d = qkv.reshape(bsz * seq, 4 * COL_TILE)
        o, lse = _dil_attn(qkv, name=f"a_attn{g}_{tag}")
        outs.append(o)
        lses.append(lse)
    acts = outs + lses + [z2d]
    specs = [_dilated_spec(a, seq) for a in outs + lses] + [_row_spec(COL_TILE, 3)]
    slabs = pltpu.VMEM((A_HEADS, ROW_TILE, LANES), F32)
    flat = pltpu.VMEM((ROW_TILE, LANES), F32)
    return _out_call(_a_out_kernel, acts, specs, x2d, w_out, g_next, final=False,
                     name=f"a_out_{tag}", scratch=(slabs, slabs, flat, flat))


def _diff_layer(x2d, hn, w_in, lam_params, subln, w_out, g_next, tables, lam_init,
                bsz, seq, final, tag):
    qscale = (B_HEAD_DIM ** -0.5) * LOG2E
    qkvz = _proj(hn.reshape(bsz, seq, D_MODEL), w_in, tables, dilation=1,
                 col_blocks=[0, 1, 2, 3], kinds=("q", "k", "v", "z"), shift=B_ROT // 2,
                 qscale=qscale, name=f"b_proj_{tag}")
    qkvz = qkvz.reshape(bsz, seq, 4 * COL_TILE)
    o = _diff_attn(qkvz, lam_params, subln, lam_init, name=f"b_attn_{tag}")
    acts = [o, qkvz.reshape(bsz * seq, 4 * COL_TILE)]
    specs = [_row_spec(D_MODEL), _row_spec(COL_TILE, 3)]
    return _out_call(functools.partial(_b_out_kernel, final=final), acts, specs, x2d, w_out,
                     g_next, final=final, name=f"b_out_{tag}")


def kernel(x, a_norm, a_w_in, a_w_out, b_norm, b_w_in, b_lambda, b_subln, b_w_out, final_norm):
    bsz, seq, _ = x.shape
    tab_a = _rope_tables(seq, A_HEAD_DIM, A_ROT)
    tab_b = _rope_tables(seq, B_HEAD_DIM, B_ROT)
    a_w_in, a_w_out = a_w_in.astype(BF16), a_w_out.astype(BF16)
    b_w_in, b_w_out = b_w_in.astype(BF16), b_w_out.astype(BF16)

    x2d = x.reshape(bsz * seq, D_MODEL)
    hn = _norm(x2d, a_norm[0])
    for i in range(DEPTH):
        j = i // 2
        if i % 2 == 0:
            x2d, hn = _dilated_layer(x2d, hn, a_w_in[j], a_w_out[j], b_norm[j], tab_a,
                                     bsz, seq, tag=str(j))
        else:
            final = i == DEPTH - 1
            g_next = final_norm if final else a_norm[j + 1]
            lam_init = 0.8 - 0.6 * math.exp(-0.3 * i)
            x2d, hn = _diff_layer(x2d, hn, b_w_in[j], b_lambda[j], b_subln[j], b_w_out[j],
                                  g_next, tab_b, lam_init, bsz, seq, final, tag=str(j))
    return x2d.reshape(bsz, seq, D_MODEL)
```

```python
import functools
import math

import jax
import jax.numpy as jnp
from jax import lax
from jax.experimental import pallas as pl
from jax.experimental.pallas import tpu as pltpu

D_MODEL = 1024
DEPTH = 4
BLK = 128
ROPE_THETA = 500000.0
EPS = 1e-6
NEG = -1e30
LOG2E = 1.4426950408889634

DILATED_GROUPS = ((128, 1), (512, 4), (2048, 16))
A_HEAD_DIM = 128
A_HEADS = 8
A_ROT = A_HEAD_DIM // 4
B_HEADS = 8
B_HEAD_DIM = 64
B_ROT = B_HEAD_DIM // 4

LANES = 128
COL_TILE = 1024
ROW_TILE = 512
PROJ_TILE = 1024
A_TQ = 512
B_TQ = 512
B_TK = 512
VMEM_LIMIT = 56 * 1024 * 1024

F32 = jnp.float32
BF16 = jnp.bfloat16


def _params(sem):
    return pltpu.CompilerParams(dimension_semantics=sem, vmem_limit_bytes=VMEM_LIMIT)


def _rope_tables(seq, head_dim, rot_dim):
    half = rot_dim // 2
    inv = 1.0 / (ROPE_THETA ** (jnp.arange(0, rot_dim, 2, dtype=F32) / rot_dim))
    ang = jnp.arange(seq, dtype=F32)[:, None] * inv[None, :]
    cos, sin = jnp.cos(ang), jnp.sin(ang)
    zeros = jnp.zeros((seq, head_dim - 2 * half), F32)
    zh = jnp.zeros((seq, half), F32)
    c = jnp.concatenate([cos, cos, jnp.ones_like(zeros)], axis=1)
    s1 = jnp.concatenate([zh, sin, zeros], axis=1)
    s2 = jnp.concatenate([-sin, zh, zeros], axis=1)
    reps = LANES // head_dim
    return tuple(jnp.tile(t, (1, reps)) for t in (c, s1, s2))


def _rms(x, g):
    ms = jnp.mean(x * x, axis=-1, keepdims=True)
    return x * lax.rsqrt(ms + EPS) * g


def _norm_kernel(x_ref, g_ref, o_ref):
    o_ref[...] = _rms(x_ref[...], g_ref[...]).astype(o_ref.dtype)


def _norm(x2d, g):
    rows = x2d.shape[0]
    return pl.pallas_call(
        _norm_kernel,
        grid=(rows // ROW_TILE,),
        in_specs=[pl.BlockSpec((ROW_TILE, D_MODEL), lambda i: (i, 0)),
                  pl.BlockSpec((1, D_MODEL), lambda i: (0, 0))],
        out_specs=pl.BlockSpec((ROW_TILE, D_MODEL), lambda i: (i, 0)),
        out_shape=jax.ShapeDtypeStruct((rows, D_MODEL), BF16),
        compiler_params=_params(("parallel",)),
        name="rmsnorm",
    )(x2d, g.reshape(1, D_MODEL))


def _strided_rows(ref, lead, p, rows, r):
    idx = pl.ds(p, rows, stride=r) if r > 1 else slice(None)
    return ref[lead + (idx, slice(None))]


_MODES = ("rope", "plain", "silu")


def _proj_kernel(tb_ref, hn_ref, w_ref, c_ref, s1_ref, s2_ref, o_ref, acc0, acc1, *,
                 r, shift, qscale):
    t = pl.program_id(0)
    tm = hn_ref.shape[0]
    rows = tm // r
    nslab = COL_TILE // LANES

    @pl.when(t == 0)
    def _():
        acc1[...] = jnp.zeros(acc1.shape, F32)

    section = tb_ref[3, t]
    mode_id = jnp.maximum(section - 1, 0)
    scale = jnp.where(section == 0, qscale, 1.0).astype(F32)

    def run(acc_new, acc_old, mode):
        acc = jnp.dot(hn_ref[...], w_ref[...], preferred_element_type=F32)
        for h in range(nslab):
            acc_new[h] = acc[:, h * LANES:(h + 1) * LANES]
        for p in range(r):
            if mode == "rope":
                c = _strided_rows(c_ref, (), p, rows, r) * scale
                s1 = _strided_rows(s1_ref, (), p, rows, r) * scale
                s2 = _strided_rows(s2_ref, (), p, rows, r) * scale
            for h in range(nslab):
                a = _strided_rows(acc_old, (h,), p, rows, r)
                if mode == "rope":
                    a = (a * c + pltpu.roll(a, shift, 1) * s1
                         + pltpu.roll(a, LANES - shift, 1) * s2)
                elif mode == "silu":
                    a = a * jax.nn.sigmoid(a)
                o_ref[p, :, h * LANES:(h + 1) * LANES] = a.astype(o_ref.dtype)

    for parity, (acc_new, acc_old) in enumerate(((acc0, acc1), (acc1, acc0))):
        for m, mode in enumerate(_MODES):
            @pl.when((t % 2 == parity) & (mode_id == m))
            def _(acc_new=acc_new, acc_old=acc_old, mode=mode):
                run(acc_new, acc_old, mode)


def _proj(hn, w_all, layer, tables, *, dilation, col_blocks, kinds, shift, qscale, name):
    assert kinds[:3] == ("q", "k", "v") and kinds[3:] in ((), ("z",))
    bsz, seq, _ = hn.shape
    r = dilation
    tm = PROJ_TILE
    ni = seq // tm
    nk = len(kinds)
    n_tiles = nk * bsz * ni

    tiles = [(j, b, i) for j in range(nk) for b in range(bsz) for i in range(ni)]
    cur = tiles + [tiles[-1]]
    prev = [tiles[0]] + tiles
    tb = jnp.asarray([[b for _, b, _ in cur], [i for _, _, i in cur],
                      [col_blocks[j] for j, _, _ in cur], [j for j, _, _ in prev],
                      [b for _, b, _ in prev], [i for _, _, i in prev]], jnp.int32)

    tab_spec = pl.BlockSpec((tm, LANES), lambda t, tb: (tb[5, t], 0))
    grid_spec = pltpu.PrefetchScalarGridSpec(
        num_scalar_prefetch=1,
        grid=(n_tiles + 1,),
        in_specs=[
            pl.BlockSpec((None, tm, D_MODEL), lambda t, tb: (tb[0, t], tb[1, t], 0)),
            pl.BlockSpec((None, D_MODEL, COL_TILE), lambda t, tb: (layer, 0, tb[2, t])),
            tab_spec, tab_spec, tab_spec,
        ],
        out_specs=pl.BlockSpec((None, r, tm // r, COL_TILE),
                               lambda t, tb: (tb[4, t], 0, tb[5, t], tb[3, t])),
        scratch_shapes=[pltpu.VMEM((COL_TILE // LANES, tm, LANES), F32),
                        pltpu.VMEM((COL_TILE // LANES, tm, LANES), F32)],
    )
    kern = functools.partial(_proj_kernel, r=r, shift=shift, qscale=qscale)
    return pl.pallas_call(
        kern,
        grid_spec=grid_spec,
        out_shape=jax.ShapeDtypeStruct((bsz, r, seq // r, nk * COL_TILE), BF16),
        compiler_params=_params(("arbitrary",)),
        name=name,
    )(tb, hn, w_all, *tables)


def _dil_attn_kernel(q_ref, kc_ref, vc_ref, kp_ref, vp_ref, o_ref, ml_ref, kf, vf):
    n = pl.program_id(2)
    kf[0:BLK] = kp_ref[...]
    kf[BLK:] = kc_ref[...]
    ones = jnp.ones((A_TQ + BLK, A_HEAD_DIM), vf.dtype)
    for h in range(A_HEADS):
        sl = slice(h * A_HEAD_DIM, (h + 1) * A_HEAD_DIM)
        vf[0:BLK, 2 * h * A_HEAD_DIM:(2 * h + 1) * A_HEAD_DIM] = vp_ref[:, sl]
        vf[BLK:, 2 * h * A_HEAD_DIM:(2 * h + 1) * A_HEAD_DIM] = vc_ref[:, sl]
        vf[:, (2 * h + 1) * A_HEAD_DIM:(2 * h + 2) * A_HEAD_DIM] = ones

    row = lax.broadcasted_iota(jnp.int32, (BLK, 2 * BLK), 0)
    col = lax.broadcasted_iota(jnp.int32, (BLK, 2 * BLK), 1)
    band = (col >= row) & (col <= row + BLK)
    band_first = band & ((col >= BLK) | (n > 0))
    lane = lax.broadcasted_iota(jnp.int32, (BLK, LANES), 1)

    for i in range(A_TQ // BLK):
        rows = slice(i * BLK, (i + 1) * BLK)
        keys = slice(i * BLK, (i + 2) * BLK)
        mask = band_first if i == 0 else band
        ml_tile = jnp.ones((BLK, LANES), F32)
        for h in range(A_HEADS):
            sl = slice(h * A_HEAD_DIM, (h + 1) * A_HEAD_DIM)
            q = q_ref[rows, sl]
            k = kf[keys, sl]
            v = vf[keys, 2 * h * A_HEAD_DIM:(2 * h + 2) * A_HEAD_DIM]
            s = lax.dot_general(q, k, (((1,), (1,)), ((), ())), preferred_element_type=F32)
            s = jnp.where(mask, s, NEG)
            m = jnp.max(s, axis=-1, keepdims=True)
            p = jnp.exp2(s - m)
            ol = jnp.dot(p.astype(BF16), v, preferred_element_type=F32)
            o_ref[rows, sl] = ol[:, :A_HEAD_DIM].astype(o_ref.dtype)
            ml_tile = jnp.where(lane == h, m, ml_tile)
            ml_tile = jnp.where(lane == A_HEADS + h, ol[:, A_HEAD_DIM:], ml_tile)
        ml_ref[rows, :] = ml_tile


def _dil_attn(qkv, *, name):
    bsz, r, length, _ = qkv.shape
    width = A_HEADS * A_HEAD_DIM
    sub = A_TQ // BLK
    blk_big = lambda c: pl.BlockSpec((None, None, A_TQ, width), lambda b, p, n: (b, p, n, c))
    blk_prev = lambda c: pl.BlockSpec(
        (None, None, BLK, width), lambda b, p, n: (b, p, jnp.maximum(n * sub - 1, 0), c))
    return pl.pallas_call(
        _dil_attn_kernel,
        grid=(bsz, r, length // A_TQ),
        in_specs=[blk_big(0), blk_big(1), blk_big(2), blk_prev(1), blk_prev(2)],
        out_specs=[pl.BlockSpec((None, None, A_TQ, width), lambda b, p, n: (b, p, n, 0)),
                   pl.BlockSpec((None, None, A_TQ, LANES), lambda b, p, n: (b, p, n, 0))],
        out_shape=[jax.ShapeDtypeStruct((bsz, r, length, width), BF16),
                   jax.ShapeDtypeStruct((bsz, r, length, LANES), F32)],
        scratch_shapes=[pltpu.VMEM((A_TQ + BLK, width), BF16),
                        pltpu.VMEM((A_TQ + BLK, 2 * width), BF16)],
        compiler_params=_params(("parallel", "parallel", "arbitrary")),
        name=name,
    )(qkv, qkv, qkv, qkv, qkv)


def _diff_attn_kernel(q_ref, k_ref, v_ref, lp_ref, g_ref, o_ref, qq, vx, m_s, acc_s, s0, s1, *,
                      lam_init):
    qi = pl.program_id(2)
    hd = 2 * B_HEAD_DIM

    @pl.when(qi == 0)
    def _():
        vx[:, :hd] = v_ref[...]
        vx[:, hd:] = jnp.ones((vx.shape[0], hd), vx.dtype)

    q = q_ref[...]
    lane = lax.broadcasted_iota(jnp.int32, q.shape, 1)
    zero = jnp.zeros_like(q)
    qq[:B_TQ] = jnp.where(lane < B_HEAD_DIM, q, zero)
    qq[B_TQ:] = jnp.where(lane >= B_HEAD_DIM, q, zero)
    m_s[...] = jnp.full(m_s.shape, NEG, F32)
    acc_s[...] = jnp.zeros(acc_s.shape, F32)
    rows2 = 2 * B_TQ

    def scores(kb, buf):
        start = pl.multiple_of(kb * B_TK, B_TK)
        k = k_ref[pl.ds(start, B_TK), :]
        buf[...] = lax.dot_general(qq[...], k, (((1,), (1,)), ((), ())),
                                   preferred_element_type=F32)

    def consume(kb, buf, diag):
        start = pl.multiple_of(kb * B_TK, B_TK)
        v = vx[pl.ds(start, B_TK), :]
        s = buf[...]
        if diag is not None:
            r_i = lax.broadcasted_iota(jnp.int32, (rows2, B_TK), 0) & (B_TQ - 1)
            c_i = lax.broadcasted_iota(jnp.int32, (rows2, B_TK), 1) + diag
            s = jnp.where(c_i <= r_i, s, NEG)
        m_prev = m_s[...]
        m_new = jnp.maximum(m_prev, jnp.max(s, axis=-1, keepdims=True))
        alpha = jnp.exp2(m_prev - m_new)
        p = jnp.exp2(s - jnp.tile(m_new, (1, B_TK // LANES)))
        pv = jnp.dot(p.astype(BF16), v, preferred_element_type=F32)
        acc_s[...] = jnp.tile(alpha, (1, 2)) * acc_s[...] + pv
        m_s[...] = m_new

    assert B_TQ == B_TK
    half = B_TK // 2
    for c0 in (0, half):
        s0[:, c0:c0 + half] = lax.dot_general(
            qq[...], k_ref[c0:c0 + half, :], (((1,), (1,)), ((), ())),
            preferred_element_type=F32)

    def pair(j, c):
        scores(2 * j + 1, s1)
        consume(2 * j, s0, None)
        scores(2 * j + 2, s0)
        consume(2 * j + 1, s1, None)
        return c

    lax.fori_loop(0, qi // 2, pair, 0)

    @pl.when(qi % 2 == 0)
    def _():
        consume(qi, s0, 0)

    @pl.when(qi % 2 == 1)
    def _():
        scores(qi, s1)
        consume(qi - 1, s0, None)
        consume(qi, s1, 0)

    lp = lp_ref[...]
    lam = (jnp.exp(jnp.sum(lp[0:1] * lp[1:2], axis=-1, keepdims=True))
           - jnp.exp(jnp.sum(lp[2:3] * lp[3:4], axis=-1, keepdims=True)) + lam_init)
    on = acc_s[:, :hd] / acc_s[:, hd:]
    o = on[:B_TQ] - lam * on[B_TQ:]
    o = _rms(o, g_ref[...]) * (1.0 - lam_init)
    o_ref[...] = o.astype(o_ref.dtype)


def _diff_attn(qkvz, lam_params, subln, lam_init, *, name):
    bsz, seq, _ = qkvz.shape
    hd = 2 * B_HEAD_DIM
    o = pl.pallas_call(
        functools.partial(_diff_attn_kernel, lam_init=lam_init),
        grid=(bsz, B_HEADS, seq // B_TQ),
        in_specs=[pl.BlockSpec((None, B_TQ, hd), lambda b, h, i: (b, i, h)),
                  pl.BlockSpec((None, seq, hd), lambda b, h, i: (b, 0, B_HEADS + h)),
                  pl.BlockSpec((None, seq, hd), lambda b, h, i: (b, 0, 2 * B_HEADS + h)),
                  pl.BlockSpec((4, B_HEAD_DIM), lambda b, h, i: (0, 0)),
                  pl.BlockSpec((1, hd), lambda b, h, i: (0, 0))],
        out_specs=pl.BlockSpec((None, B_TQ, hd), lambda b, h, i: (b, i, h)),
        out_shape=jax.ShapeDtypeStruct((bsz, seq, B_HEADS * hd), BF16),
        scratch_shapes=[pltpu.VMEM((2 * B_TQ, hd), BF16),
                        pltpu.VMEM((seq, 2 * hd), BF16),
                        pltpu.VMEM((2 * B_TQ, LANES), F32),
                        pltpu.VMEM((2 * B_TQ, 2 * hd), F32),
                        pltpu.VMEM((2 * B_TQ, B_TK), F32),
                        pltpu.VMEM((2 * B_TQ, B_TK), F32)],
        compiler_params=_params(("arbitrary", "arbitrary", "arbitrary")),
        name=name,
    )(qkvz, qkvz, qkvz, lam_params, subln.reshape(1, hd))
    return o.reshape(bsz * seq, B_HEADS * hd)


def _finish(y, x_ref, w_ref, g_ref, xo_ref, hn_ref, final):
    xn = x_ref[...] + jnp.dot(y, w_ref[...], preferred_element_type=F32)
    if final:
        xo_ref[...] = _rms(xn, g_ref[...])
    else:
        xo_ref[...] = xn
        hn_ref[...] = _rms(xn, g_ref[...]).astype(hn_ref.dtype)


def _a_out_kernel(o1, o2, o3, l1, l2, l3, z_ref, x_ref, w_ref, g_ref, xo_ref, hn_ref,
                  t2, t3, lt2, lt3):
    for o_ref, l_ref, tok, lt in ((o2, l2, t2, lt2), (o3, l3, t3, lt3)):
        r, rows = o_ref.shape[0], o_ref.shape[1]
        for p in range(r):
            idx = pl.ds(p, rows, stride=r)
            lt[idx, :] = l_ref[p]
            for h in range(A_HEADS):
                tok[h, idx, :] = o_ref[p, :, h * LANES:(h + 1) * LANES].astype(F32)
    head_lane = lax.broadcasted_iota(jnp.int32, (ROW_TILE, LANES), 1) < A_HEADS
    ms = (l1[0], lt2[...], lt3[...])
    ls = [jnp.where(head_lane, pltpu.roll(t, LANES - A_HEADS, 1), 1.0) for t in ms]
    lses = [m_g + jnp.log2(l_g) for m_g, l_g in zip(ms, ls)]
    top = jnp.maximum(jnp.maximum(lses[0], lses[1]), lses[2])
    inv = 1.0 / (jnp.exp2(lses[0] - top) + jnp.exp2(lses[1] - top) + jnp.exp2(lses[2] - top))
    width = A_HEADS * A_HEAD_DIM
    spread = (lax.broadcasted_iota(jnp.int32, (LANES, width), 0)
              == lax.broadcasted_iota(jnp.int32, (LANES, width), 1) // A_HEAD_DIM).astype(BF16)
    w1, w2, w3 = (
        jnp.dot(jnp.where(head_lane, jnp.exp2(m_g - top) * inv, 0.0).astype(BF16), spread,
                preferred_element_type=F32)
        for m_g in ms)
    pieces = []
    for h in range(A_HEADS):
        sl = slice(h * A_HEAD_DIM, (h + 1) * A_HEAD_DIM)
        o = w1[:, sl] * o1[0, :, sl].astype(F32) + w2[:, sl] * t2[h] + w3[:, sl] * t3[h]
        pieces.append((o * z_ref[:, sl].astype(F32)).astype(BF16))
    y = jnp.concatenate(pieces, axis=1)
    _finish(y, x_ref, w_ref, g_ref, xo_ref, hn_ref, False)


def _b_out_kernel(o_ref, z_ref, x_ref, w_ref, g_ref, xo_ref, *maybe_hn, final):
    y = (o_ref[...].astype(F32) * z_ref[...].astype(F32)).astype(BF16)
    _finish(y, x_ref, w_ref, g_ref, xo_ref, None if final else maybe_hn[0], final)


def _row_spec(width, col=0):
    return pl.BlockSpec((ROW_TILE, width), lambda i: (i, col))


def _dilated_spec(arr, seq):
    _, r, _, width = arr.shape
    ni = seq // ROW_TILE
    return pl.BlockSpec((None, r, ROW_TILE // r, width), lambda i: (i // ni, 0, i % ni, 0))


def _out_call(kern, acts, act_specs, x2d, w_all, layer, g, *, final, name, scratch=()):
    rows = x2d.shape[0]
    out_shape = [jax.ShapeDtypeStruct((rows, D_MODEL), F32)]
    out_specs = [_row_spec(D_MODEL)]
    if not final:
        out_shape.append(jax.ShapeDtypeStruct((rows, D_MODEL), BF16))
        out_specs.append(_row_spec(D_MODEL))
    res = pl.pallas_call(
        kern,
        grid=(rows // ROW_TILE,),
        in_specs=act_specs + [_row_spec(D_MODEL),
                              pl.BlockSpec((None, D_MODEL, D_MODEL), lambda i: (layer, 0, 0)),
                              pl.BlockSpec((1, D_MODEL), lambda i: (0, 0))],
        out_specs=out_specs,
        out_shape=out_shape,
        scratch_shapes=list(scratch),
        compiler_params=_params(("parallel",)),
        name=name,
    )(*acts, x2d, w_all, g.reshape(1, D_MODEL))
    return res if not final else (res[0], None)


def _dilated_layer(x2d, hn, w_in, w_out, layer, g_next, tables, bsz, seq, tag):
    outs, lses = [], []
    z2d = None
    qscale = (A_HEAD_DIM ** -0.5) * LOG2E
    for g, (window, r) in enumerate(DILATED_GROUPS):
        assert window // r == BLK
        with_gate = g == 0
        kinds = ("q", "k", "v", "z") if with_gate else ("q", "k", "v")
        cols = [3 * g, 3 * g + 1, 3 * g + 2] + ([9] if with_gate else [])
        qkv = _proj(hn.reshape(bsz, seq, D_MODEL), w_in, layer, tables, dilation=r, col_blocks=cols,
                    kinds=kinds, shift=A_ROT // 2, qscale=qscale, name=f"a_proj{g}_{tag}")
        if with_gate:
            z2d = qkv.reshape(bsz * seq, 4 * COL_TILE)
        o, lse = _dil_attn(qkv, name=f"a_attn{g}_{tag}")
        outs.append(o)
        lses.append(lse)
    acts = outs + lses + [z2d]
    specs = [_dilated_spec(a, seq) for a in outs + lses] + [_row_spec(COL_TILE, 3)]
    slabs = pltpu.VMEM((A_HEADS, ROW_TILE, LANES), F32)
    flat = pltpu.VMEM((ROW_TILE, LANES), F32)
    return _out_call(_a_out_kernel, acts, specs, x2d, w_out, layer, g_next, final=False,
                     name=f"a_out_{tag}", scratch=(slabs, slabs, flat, flat))


def _diff_layer(x2d, hn, w_in, lam_params, subln, w_out, layer, g_next, tables, lam_init,
                bsz, seq, final, tag):
    qscale = (B_HEAD_DIM ** -0.5) * LOG2E
    qkvz = _proj(hn.reshape(bsz, seq, D_MODEL), w_in, layer, tables, dilation=1,
                 col_blocks=[0, 1, 2, 3], kinds=("q", "k", "v", "z"), shift=B_ROT // 2,
                 qscale=qscale, name=f"b_proj_{tag}")
    qkvz = qkvz.reshape(bsz, seq, 4 * COL_TILE)
    o = _diff_attn(qkvz, lam_params, subln, lam_init, name=f"b_attn_{tag}")
    acts = [o, qkvz.reshape(bsz * seq, 4 * COL_TILE)]
    specs = [_row_spec(D_MODEL), _row_spec(COL_TILE, 3)]
    return _out_call(functools.partial(_b_out_kernel, final=final), acts, specs, x2d, w_out,
                     layer, g_next, final=final, name=f"b_out_{tag}")


def kernel(x, a_norm, a_w_in, a_w_out, b_norm, b_w_in, b_lambda, b_subln, b_w_out, final_norm):
    bsz, seq, _ = x.shape
    tab_a = _rope_tables(seq, A_HEAD_DIM, A_ROT)
    tab_b = _rope_tables(seq, B_HEAD_DIM, B_ROT)
    a_w_in, a_w_out = a_w_in.astype(BF16), a_w_out.astype(BF16)
    b_w_in, b_w_out = b_w_in.astype(BF16), b_w_out.astype(BF16)

    x2d = x.reshape(bsz * seq, D_MODEL)
    hn = _norm(x2d, a_norm[0])
    for i in range(DEPTH):
        j = i // 2
        if i % 2 == 0:
            x2d, hn = _dilated_layer(x2d, hn, a_w_in, a_w_out, j, b_norm[j], tab_a,
                                     bsz, seq, tag=str(j))
        else:
            final = i == DEPTH - 1
            g_next = final_norm if final else a_norm[j + 1]
            lam_init = 0.8 - 0.6 * math.exp(-0.3 * i)
            x2d, hn = _diff_layer(x2d, hn, b_w_in, b_lambda[j], b_subln[j], b_w_out, j,
                                  g_next, tab_b, lam_init, bsz, seq, final, tag=str(j))
    return x2d.reshape(bsz, seq, D_MODEL)
```

```python
import functools
import math

import jax
import jax.numpy as jnp
from jax import lax
from jax.experimental import pallas as pl
from jax.experimental.pallas import tpu as pltpu

D_MODEL = 1024
DEPTH = 4
BLK = 128
ROPE_THETA = 500000.0
EPS = 1e-6
NEG = -1e30
LOG2E = 1.4426950408889634

DILATED_GROUPS = ((128, 1), (512, 4), (2048, 16))
A_HEAD_DIM = 128
A_HEADS = 8
A_ROT = A_HEAD_DIM // 4
B_HEADS = 8
B_HEAD_DIM = 64
B_ROT = B_HEAD_DIM // 4

LANES = 128
COL_TILE = 1024
ROW_TILE = 512
PROJ_TILE = 1024
A_TQ = 512
B_TQ = 512
B_TK = 512
VMEM_LIMIT = 56 * 1024 * 1024

F32 = jnp.float32
BF16 = jnp.bfloat16


def _params(sem):
    return pltpu.CompilerParams(dimension_semantics=sem, vmem_limit_bytes=VMEM_LIMIT)


def _rope_tables(seq, head_dim, rot_dim):
    half = rot_dim // 2
    inv = 1.0 / (ROPE_THETA ** (jnp.arange(0, rot_dim, 2, dtype=F32) / rot_dim))
    ang = jnp.arange(seq, dtype=F32)[:, None] * inv[None, :]
    cos, sin = jnp.cos(ang), jnp.sin(ang)
    zeros = jnp.zeros((seq, head_dim - 2 * half), F32)
    zh = jnp.zeros((seq, half), F32)
    c = jnp.concatenate([cos, cos, jnp.ones_like(zeros)], axis=1)
    s1 = jnp.concatenate([zh, sin, zeros], axis=1)
    s2 = jnp.concatenate([-sin, zh, zeros], axis=1)
    reps = LANES // head_dim
    return tuple(jnp.tile(t, (1, reps)) for t in (c, s1, s2))


def _rms(x, g):
    ms = jnp.mean(x * x, axis=-1, keepdims=True)
    return x * lax.rsqrt(ms + EPS) * g


def _norm_kernel(x_ref, g_ref, o_ref):
    o_ref[...] = _rms(x_ref[...], g_ref[...]).astype(o_ref.dtype)


def _norm(x2d, g):
    rows = x2d.shape[0]
    return pl.pallas_call(
        _norm_kernel,
        grid=(rows // ROW_TILE,),
        in_specs=[pl.BlockSpec((ROW_TILE, D_MODEL), lambda i: (i, 0)),
                  pl.BlockSpec((1, D_MODEL), lambda i: (0, 0))],
        out_specs=pl.BlockSpec((ROW_TILE, D_MODEL), lambda i: (i, 0)),
        out_shape=jax.ShapeDtypeStruct((rows, D_MODEL), BF16),
        compiler_params=_params(("parallel",)),
        name="rmsnorm",
    )(x2d, g.reshape(1, D_MODEL))


def _strided_rows(ref, lead, p, rows, r):
    idx = pl.ds(p, rows, stride=r) if r > 1 else slice(None)
    return ref[lead + (idx, slice(None))]


_MODES = ("rope", "plain", "silu")


def _store_side(r):
    return r % 16 == 0


def _proj_kernel(tb_ref, hn_ref, w_ref, c_ref, s1_ref, s2_ref, o_ref, acc0, acc1, *,
                 r, shift, qscale, head_major):
    t = pl.program_id(0)
    tm = hn_ref.shape[0]
    rows = tm // r
    nslab = COL_TILE // LANES

    @pl.when(t == 0)
    def _():
        acc1[...] = jnp.zeros(acc1.shape, F32)

    section = tb_ref[3, t]
    mode_id = jnp.maximum(section - 1, 0)
    scale = jnp.where(section == 0, qscale, 1.0).astype(F32)

    pitch = rows + 8

    def run(acc_new, acc_old, mode):
        acc = jnp.dot(hn_ref[...], w_ref[...], preferred_element_type=F32)
        for h in range(nslab):
            slab = acc[:, h * LANES:(h + 1) * LANES]
            if _store_side(r):
                for l in range(rows):
                    acc_new[h, pl.ds(l, r, stride=pitch), :] = slab[l * r:(l + 1) * r]
            else:
                acc_new[h] = slab
        for p in range(r):
            if mode == "rope":
                c = _strided_rows(c_ref, (), p, rows, r) * scale
                s1 = _strided_rows(s1_ref, (), p, rows, r) * scale
                s2 = _strided_rows(s2_ref, (), p, rows, r) * scale
            for h in range(nslab):
                if _store_side(r):
                    a = acc_old[h, p * pitch:p * pitch + rows, :]
                else:
                    a = _strided_rows(acc_old, (h,), p, rows, r)
                if mode == "rope":
                    a = (a * c + pltpu.roll(a, shift, 1) * s1
                         + pltpu.roll(a, LANES - shift, 1) * s2)
                elif mode == "silu":
                    a = a * jax.nn.sigmoid(a)
                if head_major:
                    o_ref[h] = a.astype(o_ref.dtype)
                else:
                    o_ref[p, :, h * LANES:(h + 1) * LANES] = a.astype(o_ref.dtype)

    for parity, (acc_new, acc_old) in enumerate(((acc0, acc1), (acc1, acc0))):
        for m, mode in enumerate(_MODES):
            @pl.when((t % 2 == parity) & (mode_id == m))
            def _(acc_new=acc_new, acc_old=acc_old, mode=mode):
                run(acc_new, acc_old, mode)


def _proj(hn, w_all, layer, tables, *, dilation, col_blocks, kinds, shift, qscale, name,
          head_major=False):
    assert not (head_major and dilation != 1)
    assert kinds[:3] == ("q", "k", "v") and kinds[3:] in ((), ("z",))
    bsz, seq, _ = hn.shape
    r = dilation
    tm = PROJ_TILE
    ni = seq // tm
    nk = len(kinds)
    n_tiles = nk * bsz * ni

    tiles = [(j, b, i) for j in range(nk) for b in range(bsz) for i in range(ni)]
    cur = tiles + [tiles[-1]]
    prev = [tiles[0]] + tiles
    tb = jnp.asarray([[b for _, b, _ in cur], [i for _, _, i in cur],
                      [col_blocks[j] for j, _, _ in cur], [j for j, _, _ in prev],
                      [b for _, b, _ in prev], [i for _, _, i in prev]], jnp.int32)

    nslab = COL_TILE // LANES
    acc_rows = r * (tm // r + 8) if _store_side(r) else tm
    if head_major:
        out_block, out_dims = (None, None, nslab, tm, LANES), (nk, bsz, nslab, seq, LANES)
    else:
        out_block, out_dims = (None, None, r, tm // r, COL_TILE), (nk, bsz, r, seq // r, COL_TILE)
    tab_spec = pl.BlockSpec((tm, LANES), lambda t, tb: (tb[5, t], 0))
    grid_spec = pltpu.PrefetchScalarGridSpec(
        num_scalar_prefetch=1,
        grid=(n_tiles + 1,),
        in_specs=[
            pl.BlockSpec((None, tm, D_MODEL), lambda t, tb: (tb[0, t], tb[1, t], 0)),
            pl.BlockSpec((None, D_MODEL, COL_TILE), lambda t, tb: (layer, 0, tb[2, t])),
            tab_spec, tab_spec, tab_spec,
        ],
        out_specs=pl.BlockSpec(out_block, lambda t, tb: (tb[3, t], tb[4, t], 0, tb[5, t], 0)),
        scratch_shapes=[pltpu.VMEM((nslab, acc_rows, LANES), F32),
                        pltpu.VMEM((nslab, acc_rows, LANES), F32)],
    )
    kern = functools.partial(_proj_kernel, r=r, shift=shift, qscale=qscale,
                             head_major=head_major)
    return pl.pallas_call(
        kern,
        grid_spec=grid_spec,
        out_shape=jax.ShapeDtypeStruct(out_dims, BF16),
        compiler_params=_params(("arbitrary",)),
        name=name,
    )(tb, hn, w_all, *tables)


def _dil_attn_kernel(q_ref, kc_ref, vc_ref, kp_ref, vp_ref, o_ref, ml_ref, kf, vf):
    n = pl.program_id(2)
    kf[0:BLK] = kp_ref[...]
    kf[BLK:] = kc_ref[...]
    ones = jnp.ones((A_TQ + BLK, A_HEAD_DIM), vf.dtype)
    for h in range(A_HEADS):
        sl = slice(h * A_HEAD_DIM, (h + 1) * A_HEAD_DIM)
        vf[0:BLK, 2 * h * A_HEAD_DIM:(2 * h + 1) * A_HEAD_DIM] = vp_ref[:, sl]
        vf[BLK:, 2 * h * A_HEAD_DIM:(2 * h + 1) * A_HEAD_DIM] = vc_ref[:, sl]
        vf[:, (2 * h + 1) * A_HEAD_DIM:(2 * h + 2) * A_HEAD_DIM] = ones

    row = lax.broadcasted_iota(jnp.int32, (BLK, 2 * BLK), 0)
    col = lax.broadcasted_iota(jnp.int32, (BLK, 2 * BLK), 1)
    band = (col >= row) & (col <= row + BLK)
    band_first = band & ((col >= BLK) | (n > 0))
    lane = lax.broadcasted_iota(jnp.int32, (BLK, LANES), 1)

    for i in range(A_TQ // BLK):
        rows = slice(i * BLK, (i + 1) * BLK)
        keys = slice(i * BLK, (i + 2) * BLK)
        mask = band_first if i == 0 else band
        ml_tile = jnp.ones((BLK, LANES), F32)
        for h in range(A_HEADS):
            sl = slice(h * A_HEAD_DIM, (h + 1) * A_HEAD_DIM)
            q = q_ref[rows, sl]
            k = kf[keys, sl]
            v = vf[keys, 2 * h * A_HEAD_DIM:(2 * h + 2) * A_HEAD_DIM]
            s = lax.dot_general(q, k, (((1,), (1,)), ((), ())), preferred_element_type=F32)
            s = jnp.where(mask, s, NEG)
            m = jnp.max(s, axis=-1, keepdims=True)
            p = jnp.exp2(s - m)
            ol = jnp.dot(p.astype(BF16), v, preferred_element_type=F32)
            o_ref[rows, sl] = ol[:, :A_HEAD_DIM].astype(o_ref.dtype)
            ml_tile = jnp.where(lane == h, m, ml_tile)
            ml_tile = jnp.where(lane == A_HEADS + h, ol[:, A_HEAD_DIM:], ml_tile)
        ml_ref[rows, :] = ml_tile


def _dil_attn(qkv, *, name):
    _, bsz, r, length, _ = qkv.shape
    width = A_HEADS * A_HEAD_DIM
    sub = A_TQ // BLK
    blk_big = lambda c: pl.BlockSpec((None, None, None, A_TQ, width),
                                     lambda b, p, n: (c, b, p, n, 0))
    blk_prev = lambda c: pl.BlockSpec(
        (None, None, None, BLK, width),
        lambda b, p, n: (c, b, p, jnp.maximum(n * sub - 1, 0), 0))
    return pl.pallas_call(
        _dil_attn_kernel,
        grid=(bsz, r, length // A_TQ),
        in_specs=[blk_big(0), blk_big(1), blk_big(2), blk_prev(1), blk_prev(2)],
        out_specs=[pl.BlockSpec((None, None, A_TQ, width), lambda b, p, n: (b, p, n, 0)),
                   pl.BlockSpec((None, None, A_TQ, LANES), lambda b, p, n: (b, p, n, 0))],
        out_shape=[jax.ShapeDtypeStruct((bsz, r, length, width), BF16),
                   jax.ShapeDtypeStruct((bsz, r, length, LANES), F32)],
        scratch_shapes=[pltpu.VMEM((A_TQ + BLK, width), BF16),
                        pltpu.VMEM((A_TQ + BLK, 2 * width), BF16)],
        compiler_params=_params(("parallel", "parallel", "arbitrary")),
        name=name,
    )(qkv, qkv, qkv, qkv, qkv)


def _diff_attn_kernel(q_ref, k_ref, v_ref, lp_ref, g_ref, o_ref, qq, vx, m_s, acc_s, s0, s1, *,
                      lam_init):
    qi = pl.program_id(2)
    hd = 2 * B_HEAD_DIM

    @pl.when(qi == 0)
    def _():
        vx[:, :hd] = v_ref[...]
        vx[:, hd:] = jnp.ones((vx.shape[0], hd), vx.dtype)

    q = q_ref[...]
    lane = lax.broadcasted_iota(jnp.int32, q.shape, 1)
    zero = jnp.zeros_like(q)
    qq[:B_TQ] = jnp.where(lane < B_HEAD_DIM, q, zero)
    qq[B_TQ:] = jnp.where(lane >= B_HEAD_DIM, q, zero)
    m_s[...] = jnp.full(m_s.shape, NEG, F32)
    acc_s[...] = jnp.zeros(acc_s.shape, F32)
    rows2 = 2 * B_TQ

    def scores(kb, buf):
        start = pl.multiple_of(kb * B_TK, B_TK)
        k = k_ref[pl.ds(start, B_TK), :]
        buf[...] = lax.dot_general(qq[...], k, (((1,), (1,)), ((), ())),
                                   preferred_element_type=F32)

    def consume(kb, buf, diag):
        start = pl.multiple_of(kb * B_TK, B_TK)
        v = vx[pl.ds(start, B_TK), :]
        s = buf[...]
        if diag is not None:
            r_i = lax.broadcasted_iota(jnp.int32, (rows2, B_TK), 0) & (B_TQ - 1)
            c_i = lax.broadcasted_iota(jnp.int32, (rows2, B_TK), 1) + diag
            s = jnp.where(c_i <= r_i, s, NEG)
        m_prev = m_s[...]
        m_new = jnp.maximum(m_prev, jnp.max(s, axis=-1, keepdims=True))
        alpha = jnp.exp2(m_prev - m_new)
        p = jnp.exp2(s - jnp.tile(m_new, (1, B_TK // LANES)))
        pv = jnp.dot(p.astype(BF16), v, preferred_element_type=F32)
        acc_s[...] = jnp.tile(alpha, (1, 2)) * acc_s[...] + pv
        m_s[...] = m_new

    assert B_TQ == B_TK
    half = B_TK // 2
    for c0 in (0, half):
        s0[:, c0:c0 + half] = lax.dot_general(
            qq[...], k_ref[c0:c0 + half, :], (((1,), (1,)), ((), ())),
            preferred_element_type=F32)

    def pair(b0):
        scores(b0 + 1, s1)
        consume(b0, s0, None)
        scores(b0 + 2, s0)
        consume(b0 + 1, s1, None)

    def quad(j, c):
        pair(4 * j)
        pair(4 * j + 2)
        return c

    lax.fori_loop(0, qi // 4, quad, 0)

    @pl.when(qi % 4 >= 2)
    def _():
        pair(4 * (qi // 4))

    @pl.when(qi % 2 == 0)
    def _():
        consume(qi, s0, 0)

    @pl.when(qi % 2 == 1)
    def _():
        scores(qi, s1)
        consume(qi - 1, s0, None)
        consume(qi, s1, 0)

    lp = lp_ref[...]
    lam = (jnp.exp(jnp.sum(lp[0:1] * lp[1:2], axis=-1, keepdims=True))
           - jnp.exp(jnp.sum(lp[2:3] * lp[3:4], axis=-1, keepdims=True)) + lam_init)
    on = acc_s[:, :hd] / acc_s[:, hd:]
    o = on[:B_TQ] - lam * on[B_TQ:]
    o = _rms(o, g_ref[...]) * (1.0 - lam_init)
    o_ref[...] = o.astype(o_ref.dtype)


def _diff_attn(qkvz, lam_params, subln, lam_init, *, name):
    _, bsz, _, seq, _ = qkvz.shape
    hd = 2 * B_HEAD_DIM
    return pl.pallas_call(
        functools.partial(_diff_attn_kernel, lam_init=lam_init),
        grid=(bsz, B_HEADS, seq // B_TQ),
        in_specs=[pl.BlockSpec((None, None, None, B_TQ, hd), lambda b, h, i: (0, b, h, i, 0)),
                  pl.BlockSpec((None, None, None, seq, hd), lambda b, h, i: (1, b, h, 0, 0)),
                  pl.BlockSpec((None, None, None, seq, hd), lambda b, h, i: (2, b, h, 0, 0)),
                  pl.BlockSpec((4, B_HEAD_DIM), lambda b, h, i: (0, 0)),
                  pl.BlockSpec((1, hd), lambda b, h, i: (0, 0))],
        out_specs=pl.BlockSpec((None, None, B_TQ, hd), lambda b, h, i: (b, h, i, 0)),
        out_shape=jax.ShapeDtypeStruct((bsz, B_HEADS, seq, hd), BF16),
        scratch_shapes=[pltpu.VMEM((2 * B_TQ, hd), BF16),
                        pltpu.VMEM((seq, 2 * hd), BF16),
                        pltpu.VMEM((2 * B_TQ, LANES), F32),
                        pltpu.VMEM((2 * B_TQ, 2 * hd), F32),
                        pltpu.VMEM((2 * B_TQ, B_TK), F32),
                        pltpu.VMEM((2 * B_TQ, B_TK), F32)],
        compiler_params=_params(("arbitrary", "arbitrary", "arbitrary")),
        name=name,
    )(qkvz, qkvz, qkvz, lam_params, subln.reshape(1, hd))


def _finish(y, x_ref, w_ref, g_ref, xo_ref, hn_ref, final):
    xn = x_ref[...] + jnp.dot(y, w_ref[...], preferred_element_type=F32)
    if final:
        xo_ref[...] = _rms(xn, g_ref[...])
    else:
        xo_ref[...] = xn
        hn_ref[...] = _rms(xn, g_ref[...]).astype(hn_ref.dtype)


def _a_out_kernel(o1, o2, o3, l1, l2, l3, z_ref, x_ref, w_ref, g_ref, xo_ref, hn_ref,
                  t2, t3, lt2, lt3):
    for o_ref, l_ref, tok, lt in ((o2, l2, t2, lt2), (o3, l3, t3, lt3)):
        r, rows = o_ref.shape[0], o_ref.shape[1]
        for p in range(r):
            idx = pl.ds(p, rows, stride=r)
            lt[idx, :] = l_ref[p]
            for h in range(A_HEADS):
                tok[h, idx, :] = o_ref[p, :, h * LANES:(h + 1) * LANES].astype(F32)
    head_lane = lax.broadcasted_iota(jnp.int32, (ROW_TILE, LANES), 1) < A_HEADS
    ms = (l1[0], lt2[...], lt3[...])
    ls = [jnp.where(head_lane, pltpu.roll(t, LANES - A_HEADS, 1), 1.0) for t in ms]
    lses = [m_g + jnp.log2(l_g) for m_g, l_g in zip(ms, ls)]
    top = jnp.maximum(jnp.maximum(lses[0], lses[1]), lses[2])
    inv = 1.0 / (jnp.exp2(lses[0] - top) + jnp.exp2(lses[1] - top) + jnp.exp2(lses[2] - top))
    width = A_HEADS * A_HEAD_DIM
    spread = (lax.broadcasted_iota(jnp.int32, (LANES, width), 0)
              == lax.broadcasted_iota(jnp.int32, (LANES, width), 1) // A_HEAD_DIM).astype(BF16)
    w1, w2, w3 = (
        jnp.dot(jnp.where(head_lane, jnp.exp2(m_g - top) * inv, 0.0).astype(BF16), spread,
                preferred_element_type=F32)
        for m_g in ms)
    pieces = []
    for h in range(A_HEADS):
        sl = slice(h * A_HEAD_DIM, (h + 1) * A_HEAD_DIM)
        o = w1[:, sl] * o1[0, :, sl].astype(F32) + w2[:, sl] * t2[h] + w3[:, sl] * t3[h]
        pieces.append((o * z_ref[:, sl].astype(F32)).astype(BF16))
    y = jnp.concatenate(pieces, axis=1)
    _finish(y, x_ref, w_ref, g_ref, xo_ref, hn_ref, False)


def _b_out_kernel(o_ref, z_ref, x_ref, w_ref, g_ref, xo_ref, *maybe_hn, final):
    y = jnp.concatenate([(o_ref[h].astype(F32) * z_ref[h].astype(F32)).astype(BF16)
                         for h in range(B_HEADS)], axis=1)
    _finish(y, x_ref, w_ref, g_ref, xo_ref, None if final else maybe_hn[0], final)


def _row_spec(width, col=0):
    return pl.BlockSpec((ROW_TILE, width), lambda i: (i, col))


def _dilated_spec(arr, seq):
    _, r, _, width = arr.shape
    ni = seq // ROW_TILE
    return pl.BlockSpec((None, r, ROW_TILE // r, width), lambda i: (i // ni, 0, i % ni, 0))


def _out_call(kern, acts, act_specs, x2d, w_all, layer, g, *, final, name, scratch=()):
    rows = x2d.shape[0]
    out_shape = [jax.ShapeDtypeStruct((rows, D_MODEL), F32)]
    out_specs = [_row_spec(D_MODEL)]
    if not final:
        out_shape.append(jax.ShapeDtypeStruct((rows, D_MODEL), BF16))
        out_specs.append(_row_spec(D_MODEL))
    res = pl.pallas_call(
        kern,
        grid=(rows // ROW_TILE,),
        in_specs=act_specs + [_row_spec(D_MODEL),
                              pl.BlockSpec((None, D_MODEL, D_MODEL), lambda i: (layer, 0, 0)),
                              pl.BlockSpec((1, D_MODEL), lambda i: (0, 0))],
        out_specs=out_specs,
        out_shape=out_shape,
        scratch_shapes=list(scratch),
        compiler_params=_params(("parallel",)),
        name=name,
    )(*acts, x2d, w_all, g.reshape(1, D_MODEL))
    return res if not final else (res[0], None)


def _dilated_layer(x2d, hn, w_in, w_out, layer, g_next, tables, bsz, seq, tag):
    outs, lses = [], []
    gate = None
    qscale = (A_HEAD_DIM ** -0.5) * LOG2E
    for g, (window, r) in enumerate(DILATED_GROUPS):
        assert window // r == BLK
        with_gate = g == 0
        kinds = ("q", "k", "v", "z") if with_gate else ("q", "k", "v")
        cols = [3 * g, 3 * g + 1, 3 * g + 2] + ([9] if with_gate else [])
        qkv = _proj(hn.reshape(bsz, seq, D_MODEL), w_in, layer, tables, dilation=r,
                    col_blocks=cols, kinds=kinds, shift=A_ROT // 2, qscale=qscale,
                    name=f"a_proj{g}_{tag}")
        if with_gate:
            gate = qkv
        o, lse = _dil_attn(qkv, name=f"a_attn{g}_{tag}")
        outs.append(o)
        lses.append(lse)
    ni = seq // ROW_TILE
    gate_spec = pl.BlockSpec((None, None, None, ROW_TILE, COL_TILE),
                             lambda i: (3, i // ni, 0, i % ni, 0))
    acts = outs + lses + [gate]
    specs = [_dilated_spec(a, seq) for a in outs + lses] + [gate_spec]
    slabs = pltpu.VMEM((A_HEADS, ROW_TILE, LANES), F32)
    flat = pltpu.VMEM((ROW_TILE, LANES), F32)
    return _out_call(_a_out_kernel, acts, specs, x2d, w_out, layer, g_next, final=False,
                     name=f"a_out_{tag}", scratch=(slabs, slabs, flat, flat))


def _diff_layer(x2d, hn, w_in, lam_params, subln, w_out, layer, g_next, tables, lam_init,
                bsz, seq, final, tag):
    qscale = (B_HEAD_DIM ** -0.5) * LOG2E
    qkvz = _proj(hn.reshape(bsz, seq, D_MODEL), w_in, layer, tables, dilation=1,
                 col_blocks=[0, 1, 2, 3], kinds=("q", "k", "v", "z"), shift=B_ROT // 2,
                 qscale=qscale, name=f"b_proj_{tag}", head_major=True)
    o = _diff_attn(qkvz, lam_params, subln, lam_init, name=f"b_attn_{tag}")
    ni = seq // ROW_TILE
    hd = 2 * B_HEAD_DIM
    acts = [o, qkvz]
    specs = [pl.BlockSpec((None, B_HEADS, ROW_TILE, hd), lambda i: (i // ni, 0, i % ni, 0)),
             pl.BlockSpec((None, None, B_HEADS, ROW_TILE, hd),
                          lambda i: (3, i // ni, 0, i % ni, 0))]
    return _out_call(functools.partial(_b_out_kernel, final=final), acts, specs, x2d, w_out,
                     layer, g_next, final=final, name=f"b_out_{tag}")


def kernel(x, a_norm, a_w_in, a_w_out, b_norm, b_w_in, b_lambda, b_subln, b_w_out, final_norm):
    bsz, seq, _ = x.shape
    tab_a = _rope_tables(seq, A_HEAD_DIM, A_ROT)
    tab_b = _rope_tables(seq, B_HEAD_DIM, B_ROT)
    a_w_in, a_w_out = a_w_in.astype(BF16), a_w_out.astype(BF16)
    b_w_in, b_w_out = b_w_in.astype(BF16), b_w_out.astype(BF16)

    x2d = x.reshape(bsz * seq, D_MODEL)
    hn = _norm(x2d, a_norm[0])
    for i in range(DEPTH):
        j = i // 2
        if i % 2 == 0:
            x2d, hn = _dilated_layer(x2d, hn, a_w_in, a_w_out, j, b_norm[j], tab_a,
                                     bsz, seq, tag=str(j))
        else:
            final = i == DEPTH - 1
            g_next = final_norm if final else a_norm[j + 1]
            lam_init = 0.8 - 0.6 * math.exp(-0.3 * i)
            x2d, hn = _diff_layer(x2d, hn, b_w_in, b_lambda[j], b_subln[j], b_w_out, j,
                                  g_next, tab_b, lam_init, bsz, seq, final, tag=str(j))
    return x2d.reshape(bsz, seq, D_MODEL)
```

```python
import functools
import math

import jax
import jax.numpy as jnp
from jax import lax
from jax.experimental import pallas as pl
from jax.experimental.pallas import tpu as pltpu

D_MODEL = 1024
DEPTH = 4
BLK = 128
ROPE_THETA = 500000.0
EPS = 1e-6
NEG = -1e30
LOG2E = 1.4426950408889634

DILATED_GROUPS = ((128, 1), (512, 4), (2048, 16))
A_HEAD_DIM = 128
A_HEADS = 8
A_ROT = A_HEAD_DIM // 4
B_HEADS = 8
B_HEAD_DIM = 64
B_ROT = B_HEAD_DIM // 4

LANES = 128
COL_TILE = 1024
ROW_TILE = 512
PROJ_TILE = 1024
A_TQ = 512
B_TQ = 1024
B_TK = 512
VMEM_LIMIT = 56 * 1024 * 1024

F32 = jnp.float32
BF16 = jnp.bfloat16


def _params(sem):
    return pltpu.CompilerParams(dimension_semantics=sem, vmem_limit_bytes=VMEM_LIMIT)


def _rope_tables(seq, head_dim, rot_dim):
    half = rot_dim // 2
    inv = 1.0 / (ROPE_THETA ** (jnp.arange(0, rot_dim, 2, dtype=F32) / rot_dim))
    ang = jnp.arange(seq, dtype=F32)[:, None] * inv[None, :]
    cos, sin = jnp.cos(ang), jnp.sin(ang)
    zeros = jnp.zeros((seq, head_dim - 2 * half), F32)
    zh = jnp.zeros((seq, half), F32)
    c = jnp.concatenate([cos, cos, jnp.ones_like(zeros)], axis=1)
    s1 = jnp.concatenate([zh, sin, zeros], axis=1)
    s2 = jnp.concatenate([-sin, zh, zeros], axis=1)
    reps = LANES // head_dim
    return tuple(jnp.tile(t, (1, reps)) for t in (c, s1, s2))


def _rms(x, g):
    ms = jnp.mean(x * x, axis=-1, keepdims=True)
    return x * lax.rsqrt(ms + EPS) * g


def _norm_kernel(x_ref, g_ref, o_ref):
    o_ref[...] = _rms(x_ref[...], g_ref[...]).astype(o_ref.dtype)


def _norm(x2d, g):
    rows = x2d.shape[0]
    return pl.pallas_call(
        _norm_kernel,
        grid=(rows // ROW_TILE,),
        in_specs=[pl.BlockSpec((ROW_TILE, D_MODEL), lambda i: (i, 0)),
                  pl.BlockSpec((1, D_MODEL), lambda i: (0, 0))],
        out_specs=pl.BlockSpec((ROW_TILE, D_MODEL), lambda i: (i, 0)),
        out_shape=jax.ShapeDtypeStruct((rows, D_MODEL), BF16),
        compiler_params=_params(("parallel",)),
        name="rmsnorm",
    )(x2d, g.reshape(1, D_MODEL))


def _strided_rows(ref, lead, p, rows, r):
    idx = pl.ds(p, rows, stride=r) if r > 1 else slice(None)
    return ref[lead + (idx, slice(None))]


_MODES = ("rope", "plain", "silu")


def _store_side(r):
    return r % 16 == 0


def _proj_kernel(tb_ref, hn_ref, w_ref, c_ref, s1_ref, s2_ref, o_ref, acc0, acc1, *,
                 r, shift, qscale, head_major):
    t = pl.program_id(0)
    tm = hn_ref.shape[0]
    rows = tm // r
    nslab = COL_TILE // LANES

    @pl.when(t == 0)
    def _():
        acc1[...] = jnp.zeros(acc1.shape, F32)

    section = tb_ref[3, t]
    mode_id = jnp.maximum(section - 1, 0)
    scale = jnp.where(section == 0, qscale, 1.0).astype(F32)

    pitch = rows + 8

    def run(acc_new, acc_old, mode):
        acc = jnp.dot(hn_ref[...], w_ref[...].astype(BF16), preferred_element_type=F32)
        for h in range(nslab):
            slab = acc[:, h * LANES:(h + 1) * LANES]
            if _store_side(r):
                for l in range(rows):
                    acc_new[h, pl.ds(l, r, stride=pitch), :] = slab[l * r:(l + 1) * r]
            else:
                acc_new[h] = slab
        for p in range(r):
            if mode == "rope":
                c = _strided_rows(c_ref, (), p, rows, r) * scale
                s1 = _strided_rows(s1_ref, (), p, rows, r) * scale
                s2 = _strided_rows(s2_ref, (), p, rows, r) * scale
            for h in range(nslab):
                if _store_side(r):
                    a = acc_old[h, p * pitch:p * pitch + rows, :]
                else:
                    a = _strided_rows(acc_old, (h,), p, rows, r)
                if mode == "rope":
                    a = (a * c + pltpu.roll(a, shift, 1) * s1
                         + pltpu.roll(a, LANES - shift, 1) * s2)
                elif mode == "silu":
                    a = a * jax.nn.sigmoid(a)
                if head_major:
                    o_ref[h] = a.astype(o_ref.dtype)
                else:
                    o_ref[p, :, h * LANES:(h + 1) * LANES] = a.astype(o_ref.dtype)

    for parity, (acc_new, acc_old) in enumerate(((acc0, acc1), (acc1, acc0))):
        for m, mode in enumerate(_MODES):
            @pl.when((t % 2 == parity) & (mode_id == m))
            def _(acc_new=acc_new, acc_old=acc_old, mode=mode):
                run(acc_new, acc_old, mode)


def _proj(hn, w_all, layer, tables, *, dilation, col_blocks, kinds, shift, qscale, name,
          head_major=False):
    assert not (head_major and dilation != 1)
    assert kinds[:3] == ("q", "k", "v") and kinds[3:] in ((), ("z",))
    bsz, seq, _ = hn.shape
    r = dilation
    tm = PROJ_TILE
    ni = seq // tm
    nk = len(kinds)
    n_tiles = nk * bsz * ni

    tiles = [(j, b, i) for j in range(nk) for b in range(bsz) for i in range(ni)]
    cur = tiles + [tiles[-1]]
    prev = [tiles[0]] + tiles
    tb = jnp.asarray([[b for _, b, _ in cur], [i for _, _, i in cur],
                      [col_blocks[j] for j, _, _ in cur], [j for j, _, _ in prev],
                      [b for _, b, _ in prev], [i for _, _, i in prev]], jnp.int32)

    nslab = COL_TILE // LANES
    acc_rows = r * (tm // r + 8) if _store_side(r) else tm
    if head_major:
        out_block, out_dims = (None, None, nslab, tm, LANES), (nk, bsz, nslab, seq, LANES)
    else:
        out_block, out_dims = (None, None, r, tm // r, COL_TILE), (nk, bsz, r, seq // r, COL_TILE)
    tab_spec = pl.BlockSpec((tm, LANES), lambda t, tb: (tb[5, t], 0))
    grid_spec = pltpu.PrefetchScalarGridSpec(
        num_scalar_prefetch=1,
        grid=(n_tiles + 1,),
        in_specs=[
            pl.BlockSpec((None, tm, D_MODEL), lambda t, tb: (tb[0, t], tb[1, t], 0)),
            pl.BlockSpec((None, D_MODEL, COL_TILE), lambda t, tb: (layer, 0, tb[2, t])),
            tab_spec, tab_spec, tab_spec,
        ],
        out_specs=pl.BlockSpec(out_block, lambda t, tb: (tb[3, t], tb[4, t], 0, tb[5, t], 0)),
        scratch_shapes=[pltpu.VMEM((nslab, acc_rows, LANES), F32),
                        pltpu.VMEM((nslab, acc_rows, LANES), F32)],
    )
    kern = functools.partial(_proj_kernel, r=r, shift=shift, qscale=qscale,
                             head_major=head_major)
    return pl.pallas_call(
        kern,
        grid_spec=grid_spec,
        out_shape=jax.ShapeDtypeStruct(out_dims, BF16),
        compiler_params=_params(("arbitrary",)),
        name=name,
    )(tb, hn, w_all, *tables)


def _dil_attn_kernel(q_ref, kc_ref, vc_ref, kp_ref, vp_ref, o_ref, ml_ref, kf, vf):
    n = pl.program_id(2)
    kf[0:BLK] = kp_ref[...]
    kf[BLK:] = kc_ref[...]
    ones = jnp.ones((A_TQ + BLK, A_HEAD_DIM), vf.dtype)
    for h in range(A_HEADS):
        sl = slice(h * A_HEAD_DIM, (h + 1) * A_HEAD_DIM)
        vf[0:BLK, 2 * h * A_HEAD_DIM:(2 * h + 1) * A_HEAD_DIM] = vp_ref[:, sl]
        vf[BLK:, 2 * h * A_HEAD_DIM:(2 * h + 1) * A_HEAD_DIM] = vc_ref[:, sl]
        vf[:, (2 * h + 1) * A_HEAD_DIM:(2 * h + 2) * A_HEAD_DIM] = ones

    row = lax.broadcasted_iota(jnp.int32, (BLK, 2 * BLK), 0)
    col = lax.broadcasted_iota(jnp.int32, (BLK, 2 * BLK), 1)
    band = (col >= row) & (col <= row + BLK)
    band_first = band & ((col >= BLK) | (n > 0))
    lane = lax.broadcasted_iota(jnp.int32, (BLK, LANES), 1)

    for i in range(A_TQ // BLK):
        rows = slice(i * BLK, (i + 1) * BLK)
        keys = slice(i * BLK, (i + 2) * BLK)
        mask = band_first if i == 0 else band
        ml_tile = jnp.ones((BLK, LANES), F32)
        for h in range(A_HEADS):
            sl = slice(h * A_HEAD_DIM, (h + 1) * A_HEAD_DIM)
            q = q_ref[rows, sl]
            k = kf[keys, sl]
            v = vf[keys, 2 * h * A_HEAD_DIM:(2 * h + 2) * A_HEAD_DIM]
            s = lax.dot_general(q, k, (((1,), (1,)), ((), ())), preferred_element_type=F32)
            s = jnp.where(mask, s, NEG)
            m = jnp.max(s, axis=-1, keepdims=True)
            p = jnp.exp2(s - m)
            ol = jnp.dot(p.astype(BF16), v, preferred_element_type=F32)
            o_ref[rows, sl] = ol[:, :A_HEAD_DIM].astype(o_ref.dtype)
            ml_tile = jnp.where(lane == h, m, ml_tile)
            ml_tile = jnp.where(lane == A_HEADS + h, ol[:, A_HEAD_DIM:], ml_tile)
        ml_ref[rows, :] = ml_tile


def _dil_attn(qkv, *, name):
    _, bsz, r, length, _ = qkv.shape
    width = A_HEADS * A_HEAD_DIM
    sub = A_TQ // BLK
    blk_big = lambda c: pl.BlockSpec((None, None, None, A_TQ, width),
                                     lambda b, p, n: (c, b, p, n, 0))
    blk_prev = lambda c: pl.BlockSpec(
        (None, None, None, BLK, width),
        lambda b, p, n: (c, b, p, jnp.maximum(n * sub - 1, 0), 0))
    return pl.pallas_call(
        _dil_attn_kernel,
        grid=(bsz, r, length // A_TQ),
        in_specs=[blk_big(0), blk_big(1), blk_big(2), blk_prev(1), blk_prev(2)],
        out_specs=[pl.BlockSpec((None, None, A_TQ, width), lambda b, p, n: (b, p, n, 0)),
                   pl.BlockSpec((None, None, A_TQ, LANES), lambda b, p, n: (b, p, n, 0))],
        out_shape=[jax.ShapeDtypeStruct((bsz, r, length, width), BF16),
                   jax.ShapeDtypeStruct((bsz, r, length, LANES), F32)],
        scratch_shapes=[pltpu.VMEM((A_TQ + BLK, width), BF16),
                        pltpu.VMEM((A_TQ + BLK, 2 * width), BF16)],
        compiler_params=_params(("parallel", "parallel", "arbitrary")),
        name=name,
    )(qkv, qkv, qkv, qkv, qkv)


def _diff_attn_kernel(q_ref, k_ref, v_ref, lp_ref, g_ref, o_ref, qq, vx, tri, m_s, acc_s, s0, s1,
                      *, lam_init):
    qi = pl.program_id(2)
    hd = 2 * B_HEAD_DIM
    half = B_TQ // 2
    assert half == B_TK
    every = slice(0, 4 * half)
    top = slice(0, 2 * half)
    bottom = slice(2 * half, 4 * half)

    @pl.when(qi == 0)
    def _():
        vx[:, :hd] = v_ref[...]
        vx[:, hd:] = jnp.ones((vx.shape[0], hd), vx.dtype)
        r_i = lax.broadcasted_iota(jnp.int32, (half, B_TK), 0)
        c_i = lax.broadcasted_iota(jnp.int32, (half, B_TK), 1)
        tri[...] = jnp.where(c_i <= r_i, 0.0, NEG).astype(F32)

    lane = lax.broadcasted_iota(jnp.int32, (half, hd), 1)
    zero = jnp.zeros((half, hd), qq.dtype)
    for part in range(2):
        q = q_ref[part * half:(part + 1) * half, :]
        qq[(2 * part) * half:(2 * part + 1) * half] = jnp.where(lane < B_HEAD_DIM, q, zero)
        qq[(2 * part + 1) * half:(2 * part + 2) * half] = jnp.where(lane >= B_HEAD_DIM, q, zero)
    m_s[...] = jnp.full(m_s.shape, NEG, F32)
    acc_s[...] = jnp.zeros(acc_s.shape, F32)

    def scores(kb, buf, rows=every):
        start = pl.multiple_of(kb * B_TK, B_TK)
        k = k_ref[pl.ds(start, B_TK), :]
        buf[rows] = lax.dot_general(qq[rows], k, (((1,), (1,)), ((), ())),
                                    preferred_element_type=F32)

    def consume(kb, buf, rows=every, masked=None):
        start = pl.multiple_of(kb * B_TK, B_TK)
        v = vx[pl.ds(start, B_TK), :]
        if masked is None:
            s = buf[rows]
        else:
            bias = tri[...]
            parts = [buf[masked.start:masked.start + half] + bias,
                     buf[masked.start + half:masked.stop] + bias]
            if masked.stop < rows.stop:
                parts.append(buf[masked.stop:rows.stop])
            s = jnp.concatenate(parts, axis=0)
        m_prev = m_s[rows]
        m_new = jnp.maximum(m_prev, jnp.max(s, axis=-1, keepdims=True))
        alpha = jnp.exp2(m_prev - m_new)
        p = jnp.exp2(s - jnp.tile(m_new, (1, B_TK // LANES)))
        pv = jnp.dot(p.astype(BF16), v, preferred_element_type=F32)
        acc_s[rows] = jnp.tile(alpha, (1, 2)) * acc_s[rows] + pv
        m_s[rows] = m_new

    for c0 in (0, B_TK // 2):
        s0[:, c0:c0 + B_TK // 2] = lax.dot_general(
            qq[...], k_ref[c0:c0 + B_TK // 2, :], (((1,), (1,)), ((), ())),
            preferred_element_type=F32)

    def pair(b0):
        scores(b0 + 1, s1)
        consume(b0, s0)
        scores(b0 + 2, s0)
        consume(b0 + 1, s1)

    def quad(j, c):
        pair(4 * j)
        pair(4 * j + 2)
        return c

    lax.fori_loop(0, qi // 2, quad, 0)

    @pl.when(qi % 2 == 1)
    def _():
        pair(2 * qi - 2)

    scores(2 * qi + 1, s1, bottom)
    consume(2 * qi, s0, every, masked=top)
    consume(2 * qi + 1, s1, bottom, masked=bottom)

    lp = lp_ref[...]
    lam = (jnp.exp(jnp.sum(lp[0:1] * lp[1:2], axis=-1, keepdims=True))
           - jnp.exp(jnp.sum(lp[2:3] * lp[3:4], axis=-1, keepdims=True)) + lam_init)
    for part in range(2):
        lo = 2 * part * half
        on = acc_s[lo:lo + 2 * half, :hd] / acc_s[lo:lo + 2 * half, hd:]
        o = on[:half] - lam * on[half:]
        o = _rms(o, g_ref[...]) * (1.0 - lam_init)
        o_ref[part * half:(part + 1) * half, :] = o.astype(o_ref.dtype)


def _diff_attn(qkvz, lam_params, subln, lam_init, *, name):
    _, bsz, _, seq, _ = qkvz.shape
    hd = 2 * B_HEAD_DIM
    return pl.pallas_call(
        functools.partial(_diff_attn_kernel, lam_init=lam_init),
        grid=(bsz, B_HEADS, seq // B_TQ),
        in_specs=[pl.BlockSpec((None, None, None, B_TQ, hd), lambda b, h, i: (0, b, h, i, 0)),
                  pl.BlockSpec((None, None, None, seq, hd), lambda b, h, i: (1, b, h, 0, 0)),
                  pl.BlockSpec((None, None, None, seq, hd), lambda b, h, i: (2, b, h, 0, 0)),
                  pl.BlockSpec((4, B_HEAD_DIM), lambda b, h, i: (0, 0)),
                  pl.BlockSpec((1, hd), lambda b, h, i: (0, 0))],
        out_specs=pl.BlockSpec((None, None, B_TQ, hd), lambda b, h, i: (b, h, i, 0)),
        out_shape=jax.ShapeDtypeStruct((bsz, B_HEADS, seq, hd), BF16),
        scratch_shapes=[pltpu.VMEM((2 * B_TQ, hd), BF16),
                        pltpu.VMEM((seq, 2 * hd), BF16),
                        pltpu.VMEM((B_TK, B_TK), F32),
                        pltpu.VMEM((2 * B_TQ, LANES), F32),
                        pltpu.VMEM((2 * B_TQ, 2 * hd), F32),
                        pltpu.VMEM((2 * B_TQ, B_TK), F32),
                        pltpu.VMEM((2 * B_TQ, B_TK), F32)],
        compiler_params=_params(("arbitrary", "arbitrary", "arbitrary")),
        name=name,
    )(qkvz, qkvz, qkvz, lam_params, subln.reshape(1, hd))


def _finish(y, x_ref, w_ref, g_ref, xo_ref, hn_ref, final):
    xn = x_ref[...] + jnp.dot(y, w_ref[...], preferred_element_type=F32)
    if final:
        xo_ref[...] = _rms(xn, g_ref[...])
    else:
        xo_ref[...] = xn
        hn_ref[...] = _rms(xn, g_ref[...]).astype(hn_ref.dtype)


def _a_out_kernel(o1, o2, o3, l1, l2, l3, z_ref, x_ref, w_ref, g_ref, xo_ref, hn_ref,
                  t2, t3, lt2, lt3):
    for o_ref, l_ref, tok, lt in ((o2, l2, t2, lt2), (o3, l3, t3, lt3)):
        r, rows = o_ref.shape[0], o_ref.shape[1]
        for p in range(r):
            idx = pl.ds(p, rows, stride=r)
            lt[idx, :] = l_ref[p]
            for h in range(A_HEADS):
                tok[h, idx, :] = o_ref[p, :, h * LANES:(h + 1) * LANES].astype(F32)
    head_lane = lax.broadcasted_iota(jnp.int32, (ROW_TILE, LANES), 1) < A_HEADS
    ms = (l1[0], lt2[...], lt3[...])
    ls = [jnp.where(head_lane, pltpu.roll(t, LANES - A_HEADS, 1), 1.0) for t in ms]
    lses = [m_g + jnp.log2(l_g) for m_g, l_g in zip(ms, ls)]
    top = jnp.maximum(jnp.maximum(lses[0], lses[1]), lses[2])
    inv = 1.0 / (jnp.exp2(lses[0] - top) + jnp.exp2(lses[1] - top) + jnp.exp2(lses[2] - top))
    width = A_HEADS * A_HEAD_DIM
    spread = (lax.broadcasted_iota(jnp.int32, (LANES, width), 0)
              == lax.broadcasted_iota(jnp.int32, (LANES, width), 1) // A_HEAD_DIM).astype(BF16)
    w1, w2, w3 = (
        jnp.dot(jnp.where(head_lane, jnp.exp2(m_g - top) * inv, 0.0).astype(BF16), spread,
                preferred_element_type=F32)
        for m_g in ms)
    pieces = []
    for h in range(A_HEADS):
        sl = slice(h * A_HEAD_DIM, (h + 1) * A_HEAD_DIM)
        o = w1[:, sl] * o1[0, :, sl].astype(F32) + w2[:, sl] * t2[h] + w3[:, sl] * t3[h]
        pieces.append((o * z_ref[:, sl].astype(F32)).astype(BF16))
    y = jnp.concatenate(pieces, axis=1)
    _finish(y, x_ref, w_ref, g_ref, xo_ref, hn_ref, False)


def _b_out_kernel(o_ref, z_ref, x_ref, w_ref, g_ref, xo_ref, *maybe_hn, final):
    y = jnp.concatenate([(o_ref[h].astype(F32) * z_ref[h].astype(F32)).astype(BF16)
                         for h in range(B_HEADS)], axis=1)
    _finish(y, x_ref, w_ref, g_ref, xo_ref, None if final else maybe_hn[0], final)


def _row_spec(width, col=0):
    return pl.BlockSpec((ROW_TILE, width), lambda i: (i, col))


def _dilated_spec(arr, seq):
    _, r, _, width = arr.shape
    ni = seq // ROW_TILE
    return pl.BlockSpec((None, r, ROW_TILE // r, width), lambda i: (i // ni, 0, i % ni, 0))


def _out_call(kern, acts, act_specs, x2d, w_all, layer, g, *, final, name, scratch=()):
    rows = x2d.shape[0]
    out_shape = [jax.ShapeDtypeStruct((rows, D_MODEL), F32)]
    out_specs = [_row_spec(D_MODEL)]
    if not final:
        out_shape.append(jax.ShapeDtypeStruct((rows, D_MODEL), BF16))
        out_specs.append(_row_spec(D_MODEL))
    res = pl.pallas_call(
        kern,
        grid=(rows // ROW_TILE,),
        in_specs=act_specs + [_row_spec(D_MODEL),
                              pl.BlockSpec((None, D_MODEL, D_MODEL), lambda i: (layer, 0, 0)),
                              pl.BlockSpec((1, D_MODEL), lambda i: (0, 0))],
        out_specs=out_specs,
        out_shape=out_shape,
        scratch_shapes=list(scratch),
        compiler_params=_params(("parallel",)),
        name=name,
    )(*acts, x2d, w_all, g.reshape(1, D_MODEL))
    return res if not final else (res[0], None)


def _dilated_layer(x2d, hn, w_in, w_out, layer, g_next, tables, bsz, seq, tag):
    outs, lses = [], []
    gate = None
    qscale = (A_HEAD_DIM ** -0.5) * LOG2E
    for g, (window, r) in enumerate(DILATED_GROUPS):
        assert window // r == BLK
        with_gate = g == 0
        kinds = ("q", "k", "v", "z") if with_gate else ("q", "k", "v")
        cols = [3 * g, 3 * g + 1, 3 * g + 2] + ([9] if with_gate else [])
        qkv = _proj(hn.reshape(bsz, seq, D_MODEL), w_in, layer, tables, dilation=r,
                    col_blocks=cols, kinds=kinds, shift=A_ROT // 2, qscale=qscale,
                    name=f"a_proj{g}_{tag}")
        if with_gate:
            gate = qkv
        o, lse = _dil_attn(qkv, name=f"a_attn{g}_{tag}")
        outs.append(o)
        lses.append(lse)
    ni = seq // ROW_TILE
    gate_spec = pl.BlockSpec((None, None, None, ROW_TILE, COL_TILE),
                             lambda i: (3, i // ni, 0, i % ni, 0))
    acts = outs + lses + [gate]
    specs = [_dilated_spec(a, seq) for a in outs + lses] + [gate_spec]
    slabs = pltpu.VMEM((A_HEADS, ROW_TILE, LANES), F32)
    flat = pltpu.VMEM((ROW_TILE, LANES), F32)
    return _out_call(_a_out_kernel, acts, specs, x2d, w_out, layer, g_next, final=False,
                     name=f"a_out_{tag}", scratch=(slabs, slabs, flat, flat))


def _diff_layer(x2d, hn, w_in, lam_params, subln, w_out, layer, g_next, tables, lam_init,
                bsz, seq, final, tag):
    qscale = (B_HEAD_DIM ** -0.5) * LOG2E
    qkvz = _proj(hn.reshape(bsz, seq, D_MODEL), w_in, layer, tables, dilation=1,
                 col_blocks=[0, 1, 2, 3], kinds=("q", "k", "v", "z"), shift=B_ROT // 2,
                 qscale=qscale, name=f"b_proj_{tag}", head_major=True)
    o = _diff_attn(qkvz, lam_params, subln, lam_init, name=f"b_attn_{tag}")
    ni = seq // ROW_TILE
    hd = 2 * B_HEAD_DIM
    acts = [o, qkvz]
    specs = [pl.BlockSpec((None, B_HEADS, ROW_TILE, hd), lambda i: (i // ni, 0, i % ni, 0)),
             pl.BlockSpec((None, None, B_HEADS, ROW_TILE, hd),
                          lambda i: (3, i // ni, 0, i % ni, 0))]
    return _out_call(functools.partial(_b_out_kernel, final=final), acts, specs, x2d, w_out,
                     layer, g_next, final=final, name=f"b_out_{tag}")


def kernel(x, a_norm, a_w_in, a_w_out, b_norm, b_w_in, b_lambda, b_subln, b_w_out, final_norm):
    bsz, seq, _ = x.shape
    tab_a = _rope_tables(seq, A_HEAD_DIM, A_ROT)
    tab_b = _rope_tables(seq, B_HEAD_DIM, B_ROT)
    a_w_out, b_w_out = a_w_out.astype(BF16), b_w_out.astype(BF16)

    x2d = x.reshape(bsz * seq, D_MODEL)
    hn = _norm(x2d, a_norm[0])
    for i in range(DEPTH):
        j = i // 2
        if i % 2 == 0:
            x2d, hn = _dilated_layer(x2d, hn, a_w_in, a_w_out, j, b_norm[j], tab_a,
                                     bsz, seq, tag=str(j))
        else:
            final = i == DEPTH - 1
            g_next = final_norm if final else a_norm[j + 1]
            lam_init = 0.8 - 0.6 * math.exp(-0.3 * i)
            x2d, hn = _diff_layer(x2d, hn, b_w_in, b_lambda[j], b_subln[j], b_w_out, j,
                                  g_next, tab_b, lam_init, bsz, seq, final, tag=str(j))
    return x2d.reshape(bsz, seq, D_MODEL)
```

```python
import functools
import math

import jax
import jax.numpy as jnp
from jax import lax
from jax.experimental import pallas as pl
from jax.experimental.pallas import tpu as pltpu

D_MODEL = 1024
DEPTH = 4
BLK = 128
ROPE_THETA = 500000.0
EPS = 1e-6
NEG = -1e30
LOG2E = 1.4426950408889634

DILATED_GROUPS = ((128, 1), (512, 4), (2048, 16))
A_HEAD_DIM = 128
A_HEADS = 8
A_ROT = A_HEAD_DIM // 4
B_HEADS = 8
B_HEAD_DIM = 64
B_ROT = B_HEAD_DIM // 4

LANES = 128
COL_TILE = 1024
ROW_TILE = 512
PROJ_TILE = 1024
A_TQ = 512
B_TQ = 1024
B_TK = 512
VMEM_LIMIT = 56 * 1024 * 1024

F32 = jnp.float32
BF16 = jnp.bfloat16


def _params(sem):
    return pltpu.CompilerParams(dimension_semantics=sem, vmem_limit_bytes=VMEM_LIMIT)


def _rope_tables(seq, head_dim, rot_dim):
    half = rot_dim // 2
    inv = 1.0 / (ROPE_THETA ** (jnp.arange(0, rot_dim, 2, dtype=F32) / rot_dim))
    ang = jnp.arange(seq, dtype=F32)[:, None] * inv[None, :]
    cos, sin = jnp.cos(ang), jnp.sin(ang)
    zeros = jnp.zeros((seq, head_dim - 2 * half), F32)
    zh = jnp.zeros((seq, half), F32)
    c = jnp.concatenate([cos, cos, jnp.ones_like(zeros)], axis=1)
    s1 = jnp.concatenate([zh, sin, zeros], axis=1)
    s2 = jnp.concatenate([-sin, zh, zeros], axis=1)
    reps = LANES // head_dim
    return tuple(jnp.tile(t, (1, reps)) for t in (c, s1, s2))


def _rms(x, g):
    ms = jnp.mean(x * x, axis=-1, keepdims=True)
    return x * lax.rsqrt(ms + EPS) * g


def _norm_kernel(x_ref, g_ref, o_ref):
    o_ref[...] = _rms(x_ref[...], g_ref[...]).astype(o_ref.dtype)


def _norm(x2d, g):
    rows = x2d.shape[0]
    return pl.pallas_call(
        _norm_kernel,
        grid=(rows // ROW_TILE,),
        in_specs=[pl.BlockSpec((ROW_TILE, D_MODEL), lambda i: (i, 0)),
                  pl.BlockSpec((1, D_MODEL), lambda i: (0, 0))],
        out_specs=pl.BlockSpec((ROW_TILE, D_MODEL), lambda i: (i, 0)),
        out_shape=jax.ShapeDtypeStruct((rows, D_MODEL), BF16),
        compiler_params=_params(("parallel",)),
        name="rmsnorm",
    )(x2d, g.reshape(1, D_MODEL))


def _strided_rows(ref, lead, p, rows, r):
    idx = pl.ds(p, rows, stride=r) if r > 1 else slice(None)
    return ref[lead + (idx, slice(None))]


_MODES = ("rope", "plain", "silu")


def _store_side(r):
    return r % 16 == 0


def _proj_kernel(tb_ref, hn_ref, w_ref, c_ref, s1_ref, s2_ref, o_ref, acc0, acc1, *,
                 r, shift, qscale, head_major):
    t = pl.program_id(0)
    tm = hn_ref.shape[0]
    rows = tm // r
    nslab = COL_TILE // LANES

    @pl.when(t == 0)
    def _():
        acc1[...] = jnp.zeros(acc1.shape, F32)

    section = tb_ref[3, t]
    mode_id = jnp.maximum(section - 1, 0)
    scale = jnp.where(section == 0, qscale, 1.0).astype(F32)

    pitch = rows + 8

    def matmul(acc_new):
        acc = jnp.dot(hn_ref[...], w_ref[...].astype(BF16), preferred_element_type=F32)
        for h in range(nslab):
            slab = acc[:, h * LANES:(h + 1) * LANES]
            if _store_side(r):
                for l in range(rows):
                    acc_new[h, pl.ds(l, r, stride=pitch), :] = slab[l * r:(l + 1) * r]
            else:
                acc_new[h] = slab

    def epilogue(acc_old, mode):
        for p in range(r):
            if mode == "rope":
                c = _strided_rows(c_ref, (), p, rows, r) * scale
                s1 = _strided_rows(s1_ref, (), p, rows, r) * scale
                s2 = _strided_rows(s2_ref, (), p, rows, r) * scale
            for h in range(nslab):
                if _store_side(r):
                    a = acc_old[h, p * pitch:p * pitch + rows, :]
                else:
                    a = _strided_rows(acc_old, (h,), p, rows, r)
                if mode == "rope":
                    a = (a * c + pltpu.roll(a, shift, 1) * s1
                         + pltpu.roll(a, LANES - shift, 1) * s2)
                elif mode == "silu":
                    a = a * jax.nn.sigmoid(a)
                if head_major:
                    o_ref[h] = a.astype(o_ref.dtype)
                else:
                    o_ref[p, :, h * LANES:(h + 1) * LANES] = a.astype(o_ref.dtype)

    def run(acc_new, acc_old, mode):
        if mode == "silu":
            epilogue(acc_old, mode)
            matmul(acc_new)
        else:
            matmul(acc_new)
            epilogue(acc_old, mode)

    for parity, (acc_new, acc_old) in enumerate(((acc0, acc1), (acc1, acc0))):
        for m, mode in enumerate(_MODES):
            @pl.when((t % 2 == parity) & (mode_id == m))
            def _(acc_new=acc_new, acc_old=acc_old, mode=mode):
                run(acc_new, acc_old, mode)


def _proj(hn, w_all, layer, tables, *, dilation, col_blocks, kinds, shift, qscale, name,
          head_major=False):
    assert not (head_major and dilation != 1)
    assert kinds[:3] == ("q", "k", "v") and kinds[3:] in ((), ("z",))
    bsz, seq, _ = hn.shape
    r = dilation
    tm = PROJ_TILE
    ni = seq // tm
    nk = len(kinds)
    n_tiles = nk * bsz * ni

    tiles = [(j, b, i) for j in range(nk) for b in range(bsz) for i in range(ni)]
    cur = tiles + [tiles[-1]]
    prev = [tiles[0]] + tiles
    tb = jnp.asarray([[b for _, b, _ in cur], [i for _, _, i in cur],
                      [col_blocks[j] for j, _, _ in cur], [j for j, _, _ in prev],
                      [b for _, b, _ in prev], [i for _, _, i in prev]], jnp.int32)

    nslab = COL_TILE // LANES
    acc_rows = r * (tm // r + 8) if _store_side(r) else tm
    if head_major:
        out_block, out_dims = (None, None, nslab, tm, LANES), (nk, bsz, nslab, seq, LANES)
    else:
        out_block, out_dims = (None, None, r, tm // r, COL_TILE), (nk, bsz, r, seq // r, COL_TILE)
    tab_spec = pl.BlockSpec((tm, LANES), lambda t, tb: (tb[5, t], 0))
    grid_spec = pltpu.PrefetchScalarGridSpec(
        num_scalar_prefetch=1,
        grid=(n_tiles + 1,),
        in_specs=[
            pl.BlockSpec((None, tm, D_MODEL), lambda t, tb: (tb[0, t], tb[1, t], 0)),
            pl.BlockSpec((None, D_MODEL, COL_TILE), lambda t, tb: (layer, 0, tb[2, t])),
            tab_spec, tab_spec, tab_spec,
        ],
        out_specs=pl.BlockSpec(out_block, lambda t, tb: (tb[3, t], tb[4, t], 0, tb[5, t], 0)),
        scratch_shapes=[pltpu.VMEM((nslab, acc_rows, LANES), F32),
                        pltpu.VMEM((nslab, acc_rows, LANES), F32)],
    )
    kern = functools.partial(_proj_kernel, r=r, shift=shift, qscale=qscale,
                             head_major=head_major)
    return pl.pallas_call(
        kern,
        grid_spec=grid_spec,
        out_shape=jax.ShapeDtypeStruct(out_dims, BF16),
        compiler_params=_params(("arbitrary",)),
        name=name,
    )(tb, hn, w_all, *tables)


def _dil_attn_kernel(q_ref, kc_ref, vc_ref, kp_ref, vp_ref, o_ref, ml_ref, kf, vf):
    n = pl.program_id(2)
    kf[0:BLK] = kp_ref[...]
    kf[BLK:] = kc_ref[...]
    ones = jnp.ones((A_TQ + BLK, A_HEAD_DIM), vf.dtype)
    for h in range(A_HEADS):
        sl = slice(h * A_HEAD_DIM, (h + 1) * A_HEAD_DIM)
        vf[0:BLK, 2 * h * A_HEAD_DIM:(2 * h + 1) * A_HEAD_DIM] = vp_ref[:, sl]
        vf[BLK:, 2 * h * A_HEAD_DIM:(2 * h + 1) * A_HEAD_DIM] = vc_ref[:, sl]
        vf[:, (2 * h + 1) * A_HEAD_DIM:(2 * h + 2) * A_HEAD_DIM] = ones

    row = lax.broadcasted_iota(jnp.int32, (BLK, 2 * BLK), 0)
    col = lax.broadcasted_iota(jnp.int32, (BLK, 2 * BLK), 1)
    band = (col >= row) & (col <= row + BLK)
    band_first = band & ((col >= BLK) | (n > 0))
    lane = lax.broadcasted_iota(jnp.int32, (BLK, LANES), 1)

    for i in range(A_TQ // BLK):
        rows = slice(i * BLK, (i + 1) * BLK)
        keys = slice(i * BLK, (i + 2) * BLK)
        mask = band_first if i == 0 else band
        ml_tile = jnp.ones((BLK, LANES), F32)
        for h in range(A_HEADS):
            sl = slice(h * A_HEAD_DIM, (h + 1) * A_HEAD_DIM)
            q = q_ref[rows, sl]
            k = kf[keys, sl]
            v = vf[keys, 2 * h * A_HEAD_DIM:(2 * h + 2) * A_HEAD_DIM]
            s = lax.dot_general(q, k, (((1,), (1,)), ((), ())), preferred_element_type=F32)
            s = jnp.where(mask, s, NEG)
            m = jnp.max(s, axis=-1, keepdims=True)
            p = jnp.exp2(s - m)
            ol = jnp.dot(p.astype(BF16), v, preferred_element_type=F32)
            o_ref[rows, sl] = ol[:, :A_HEAD_DIM].astype(o_ref.dtype)
            ml_tile = jnp.where(lane == h, m, ml_tile)
            ml_tile = jnp.where(lane == A_HEADS + h, ol[:, A_HEAD_DIM:], ml_tile)
        ml_ref[rows, :] = ml_tile


def _dil_attn(qkv, *, name):
    _, bsz, r, length, _ = qkv.shape
    width = A_HEADS * A_HEAD_DIM
    sub = A_TQ // BLK
    blk_big = lambda c: pl.BlockSpec((None, None, None, A_TQ, width),
                                     lambda b, p, n: (c, b, p, n, 0))
    blk_prev = lambda c: pl.BlockSpec(
        (None, None, None, BLK, width),
        lambda b, p, n: (c, b, p, jnp.maximum(n * sub - 1, 0), 0))
    return pl.pallas_call(
        _dil_attn_kernel,
        grid=(bsz, r, length // A_TQ),
        in_specs=[blk_big(0), blk_big(1), blk_big(2), blk_prev(1), blk_prev(2)],
        out_specs=[pl.BlockSpec((None, None, A_TQ, width), lambda b, p, n: (b, p, n, 0)),
                   pl.BlockSpec((None, None, A_TQ, LANES), lambda b, p, n: (b, p, n, 0))],
        out_shape=[jax.ShapeDtypeStruct((bsz, r, length, width), BF16),
                   jax.ShapeDtypeStruct((bsz, r, length, LANES), F32)],
        scratch_shapes=[pltpu.VMEM((A_TQ + BLK, width), BF16),
                        pltpu.VMEM((A_TQ + BLK, 2 * width), BF16)],
        compiler_params=_params(("parallel", "parallel", "arbitrary")),
        name=name,
    )(qkv, qkv, qkv, qkv, qkv)


def _diff_attn_kernel(q_ref, k_ref, v_ref, lp_ref, g_ref, o_ref, qq, vx, tri, m_s, acc_s, s0, s1,
                      *, lam_init):
    qi = pl.program_id(2)
    hd = 2 * B_HEAD_DIM
    half = B_TQ // 2
    assert half == B_TK
    every = slice(0, 4 * half)
    top = slice(0, 2 * half)
    bottom = slice(2 * half, 4 * half)

    @pl.when(qi == 0)
    def _():
        vx[:, :hd] = v_ref[...]
        vx[:, hd:] = jnp.ones((vx.shape[0], hd), vx.dtype)
        r_i = lax.broadcasted_iota(jnp.int32, (half, B_TK), 0)
        c_i = lax.broadcasted_iota(jnp.int32, (half, B_TK), 1)
        tri[...] = jnp.where(c_i <= r_i, 0.0, NEG).astype(F32)

    lane = lax.broadcasted_iota(jnp.int32, (half, hd), 1)
    zero = jnp.zeros((half, hd), qq.dtype)
    for part in range(2):
        q = q_ref[part * half:(part + 1) * half, :]
        qq[(2 * part) * half:(2 * part + 1) * half] = jnp.where(lane < B_HEAD_DIM, q, zero)
        qq[(2 * part + 1) * half:(2 * part + 2) * half] = jnp.where(lane >= B_HEAD_DIM, q, zero)
    m_s[...] = jnp.full(m_s.shape, NEG, F32)
    acc_s[...] = jnp.zeros(acc_s.shape, F32)

    def scores(kb, buf, rows=every):
        start = pl.multiple_of(kb * B_TK, B_TK)
        k = k_ref[pl.ds(start, B_TK), :]
        buf[rows] = lax.dot_general(qq[rows], k, (((1,), (1,)), ((), ())),
                                    preferred_element_type=F32)

    def consume(kb, buf, rows=every, masked=None):
        start = pl.multiple_of(kb * B_TK, B_TK)
        v = vx[pl.ds(start, B_TK), :]
        if masked is None:
            s = buf[rows]
        else:
            bias = tri[...]
            parts = [buf[masked.start:masked.start + half] + bias,
                     buf[masked.start + half:masked.stop] + bias]
            if masked.stop < rows.stop:
                parts.append(buf[masked.stop:rows.stop])
            s = jnp.concatenate(parts, axis=0)
        m_prev = m_s[rows]
        m_new = jnp.maximum(m_prev, jnp.max(s, axis=-1, keepdims=True))
        alpha = jnp.exp2(m_prev - m_new)
        p = jnp.exp2(s - jnp.tile(m_new, (1, B_TK // LANES)))
        pv = jnp.dot(p.astype(BF16), v, preferred_element_type=F32)
        acc_s[rows] = jnp.tile(alpha, (1, 2)) * acc_s[rows] + pv
        m_s[rows] = m_new

    for c0 in (0, B_TK // 2):
        s0[:, c0:c0 + B_TK // 2] = lax.dot_general(
            qq[...], k_ref[c0:c0 + B_TK // 2, :], (((1,), (1,)), ((), ())),
            preferred_element_type=F32)

    def pair(b0):
        scores(b0 + 1, s1)
        consume(b0, s0)
        scores(b0 + 2, s0)
        consume(b0 + 1, s1)

    def quad(j, c):
        pair(4 * j)
        pair(4 * j + 2)
        return c

    lax.fori_loop(0, qi // 2, quad, 0)

    @pl.when(qi % 2 == 1)
    def _():
        pair(2 * qi - 2)

    scores(2 * qi + 1, s1, bottom)
    consume(2 * qi, s0, every, masked=top)
    consume(2 * qi + 1, s1, bottom, masked=bottom)

    lp = lp_ref[...]
    lam = (jnp.exp(jnp.sum(lp[0:1] * lp[1:2], axis=-1, keepdims=True))
           - jnp.exp(jnp.sum(lp[2:3] * lp[3:4], axis=-1, keepdims=True)) + lam_init)
    for part in range(2):
        lo = 2 * part * half
        on = acc_s[lo:lo + 2 * half, :hd] / acc_s[lo:lo + 2 * half, hd:]
        o = on[:half] - lam * on[half:]
        o = _rms(o, g_ref[...]) * (1.0 - lam_init)
        o_ref[part * half:(part + 1) * half, :] = o.astype(o_ref.dtype)


def _diff_attn(qkvz, lam_params, subln, lam_init, *, name):
    _, bsz, _, seq, _ = qkvz.shape
    hd = 2 * B_HEAD_DIM
    return pl.pallas_call(
        functools.partial(_diff_attn_kernel, lam_init=lam_init),
        grid=(bsz, B_HEADS, seq // B_TQ),
        in_specs=[pl.BlockSpec((None, None, None, B_TQ, hd), lambda b, h, i: (0, b, h, i, 0)),
                  pl.BlockSpec((None, None, None, seq, hd), lambda b, h, i: (1, b, h, 0, 0)),
                  pl.BlockSpec((None, None, None, seq, hd), lambda b, h, i: (2, b, h, 0, 0)),
                  pl.BlockSpec((4, B_HEAD_DIM), lambda b, h, i: (0, 0)),
                  pl.BlockSpec((1, hd), lambda b, h, i: (0, 0))],
        out_specs=pl.BlockSpec((None, None, B_TQ, hd), lambda b, h, i: (b, h, i, 0)),
        out_shape=jax.ShapeDtypeStruct((bsz, B_HEADS, seq, hd), BF16),
        scratch_shapes=[pltpu.VMEM((2 * B_TQ, hd), BF16),
                        pltpu.VMEM((seq, 2 * hd), BF16),
                        pltpu.VMEM((B_TK, B_TK), F32),
                        pltpu.VMEM((2 * B_TQ, LANES), F32),
                        pltpu.VMEM((2 * B_TQ, 2 * hd), F32),
                        pltpu.VMEM((2 * B_TQ, B_TK), F32),
                        pltpu.VMEM((2 * B_TQ, B_TK), F32)],
        compiler_params=_params(("arbitrary", "arbitrary", "arbitrary")),
        name=name,
    )(qkvz, qkvz, qkvz, lam_params, subln.reshape(1, hd))


def _finish(y, x_ref, w_ref, g_ref, xo_ref, hn_ref, final):
    xn = x_ref[...] + jnp.dot(y, w_ref[...], preferred_element_type=F32)
    if final:
        xo_ref[...] = _rms(xn, g_ref[...])
    else:
        xo_ref[...] = xn
        hn_ref[...] = _rms(xn, g_ref[...]).astype(hn_ref.dtype)


def _a_out_kernel(o1, o2, o3, l1, l2, l3, z_ref, x_ref, w_ref, g_ref, xo_ref, hn_ref,
                  t2, t3, lt2, lt3):
    for o_ref, l_ref, tok, lt in ((o2, l2, t2, lt2), (o3, l3, t3, lt3)):
        r, rows = o_ref.shape[0], o_ref.shape[1]
        for p in range(r):
            idx = pl.ds(p, rows, stride=r)
            lt[idx, :] = l_ref[p]
            for h in range(A_HEADS):
                tok[h, idx, :] = o_ref[p, :, h * LANES:(h + 1) * LANES].astype(F32)
    head_lane = lax.broadcasted_iota(jnp.int32, (ROW_TILE, LANES), 1) < A_HEADS
    ms = (l1[0], lt2[...], lt3[...])
    ls = [jnp.where(head_lane, pltpu.roll(t, LANES - A_HEADS, 1), 1.0) for t in ms]
    lses = [m_g + jnp.log2(l_g) for m_g, l_g in zip(ms, ls)]
    top = jnp.maximum(jnp.maximum(lses[0], lses[1]), lses[2])
    inv = 1.0 / (jnp.exp2(lses[0] - top) + jnp.exp2(lses[1] - top) + jnp.exp2(lses[2] - top))
    width = A_HEADS * A_HEAD_DIM
    spread = (lax.broadcasted_iota(jnp.int32, (LANES, width), 0)
              == lax.broadcasted_iota(jnp.int32, (LANES, width), 1) // A_HEAD_DIM).astype(BF16)
    w1, w2, w3 = (
        jnp.dot(jnp.where(head_lane, jnp.exp2(m_g - top) * inv, 0.0).astype(BF16), spread,
                preferred_element_type=F32)
        for m_g in ms)
    pieces = []
    for h in range(A_HEADS):
        sl = slice(h * A_HEAD_DIM, (h + 1) * A_HEAD_DIM)
        o = w1[:, sl] * o1[0, :, sl].astype(F32) + w2[:, sl] * t2[h] + w3[:, sl] * t3[h]
        pieces.append((o * z_ref[:, sl].astype(F32)).astype(BF16))
    y = jnp.concatenate(pieces, axis=1)
    _finish(y, x_ref, w_ref, g_ref, xo_ref, hn_ref, False)


def _b_out_kernel(o_ref, z_ref, x_ref, w_ref, g_ref, xo_ref, *maybe_hn, final):
    y = jnp.concatenate([(o_ref[h].astype(F32) * z_ref[h].astype(F32)).astype(BF16)
                         for h in range(B_HEADS)], axis=1)
    _finish(y, x_ref, w_ref, g_ref, xo_ref, None if final else maybe_hn[0], final)


def _row_spec(width, col=0):
    return pl.BlockSpec((ROW_TILE, width), lambda i: (i, col))


def _dilated_spec(arr, seq):
    _, r, _, width = arr.shape
    ni = seq // ROW_TILE
    return pl.BlockSpec((None, r, ROW_TILE // r, width), lambda i: (i // ni, 0, i % ni, 0))


def _out_call(kern, acts, act_specs, x2d, w_all, layer, g, *, final, name, scratch=()):
    rows = x2d.shape[0]
    out_shape = [jax.ShapeDtypeStruct((rows, D_MODEL), F32)]
    out_specs = [_row_spec(D_MODEL)]
    if not final:
        out_shape.append(jax.ShapeDtypeStruct((rows, D_MODEL), BF16))
        out_specs.append(_row_spec(D_MODEL))
    res = pl.pallas_call(
        kern,
        grid=(rows // ROW_TILE,),
        in_specs=act_specs + [_row_spec(D_MODEL),
                              pl.BlockSpec((None, D_MODEL, D_MODEL), lambda i: (layer, 0, 0)),
                              pl.BlockSpec((1, D_MODEL), lambda i: (0, 0))],
        out_specs=out_specs,
        out_shape=out_shape,
        scratch_shapes=list(scratch),
        compiler_params=_params(("parallel",)),
        name=name,
    )(*acts, x2d, w_all, g.reshape(1, D_MODEL))
    return res if not final else (res[0], None)


def _dilated_layer(x2d, hn, w_in, w_out, layer, g_next, tables, bsz, seq, tag):
    outs, lses = [], []
    gate = None
    qscale = (A_HEAD_DIM ** -0.5) * LOG2E
    for g, (window, r) in enumerate(DILATED_GROUPS):
        assert window // r == BLK
        with_gate = g == 0
        kinds = ("q", "k", "v", "z") if with_gate else ("q", "k", "v")
        cols = [3 * g, 3 * g + 1, 3 * g + 2] + ([9] if with_gate else [])
        qkv = _proj(hn.reshape(bsz, seq, D_MODEL), w_in, layer, tables, dilation=r,
                    col_blocks=cols, kinds=kinds, shift=A_ROT // 2, qscale=qscale,
                    name=f"a_proj{g}_{tag}")
        if with_gate:
            gate = qkv
        o, lse = _dil_attn(qkv, name=f"a_attn{g}_{tag}")
        outs.append(o)
        lses.append(lse)
    ni = seq // ROW_TILE
    gate_spec = pl.BlockSpec((None, None, None, ROW_TILE, COL_TILE),
                             lambda i: (3, i // ni, 0, i % ni, 0))
    acts = outs + lses + [gate]
    specs = [_dilated_spec(a, seq) for a in outs + lses] + [gate_spec]
    slabs = pltpu.VMEM((A_HEADS, ROW_TILE, LANES), F32)
    flat = pltpu.VMEM((ROW_TILE, LANES), F32)
    return _out_call(_a_out_kernel, acts, specs, x2d, w_out, layer, g_next, final=False,
                     name=f"a_out_{tag}", scratch=(slabs, slabs, flat, flat))


def _diff_layer(x2d, hn, w_in, lam_params, subln, w_out, layer, g_next, tables, lam_init,
                bsz, seq, final, tag):
    qscale = (B_HEAD_DIM ** -0.5) * LOG2E
    qkvz = _proj(hn.reshape(bsz, seq, D_MODEL), w_in, layer, tables, dilation=1,
                 col_blocks=[0, 1, 2, 3], kinds=("q", "k", "v", "z"), shift=B_ROT // 2,
                 qscale=qscale, name=f"b_proj_{tag}", head_major=True)
    o = _diff_attn(qkvz, lam_params, subln, lam_init, name=f"b_attn_{tag}")
    ni = seq // ROW_TILE
    hd = 2 * B_HEAD_DIM
    acts = [o, qkvz]
    specs = [pl.BlockSpec((None, B_HEADS, ROW_TILE, hd), lambda i: (i // ni, 0, i % ni, 0)),
             pl.BlockSpec((None, None, B_HEADS, ROW_TILE, hd),
                          lambda i: (3, i // ni, 0, i % ni, 0))]
    return _out_call(functools.partial(_b_out_kernel, final=final), acts, specs, x2d, w_out,
                     layer, g_next, final=final, name=f"b_out_{tag}")


def kernel(x, a_norm, a_w_in, a_w_out, b_norm, b_w_in, b_lambda, b_subln, b_w_out, final_norm):
    bsz, seq, _ = x.shape
    tab_a = _rope_tables(seq, A_HEAD_DIM, A_ROT)
    tab_b = _rope_tables(seq, B_HEAD_DIM, B_ROT)
    a_w_out, b_w_out = a_w_out.astype(BF16), b_w_out.astype(BF16)

    x2d = x.reshape(bsz * seq, D_MODEL)
    hn = _norm(x2d, a_norm[0])
    for i in range(DEPTH):
        j = i // 2
        if i % 2 == 0:
            x2d, hn = _dilated_layer(x2d, hn, a_w_in, a_w_out, j, b_norm[j], tab_a,
                                     bsz, seq, tag=str(j))
        else:
            final = i == DEPTH - 1
            g_next = final_norm if final else a_norm[j + 1]
            lam_init = 0.8 - 0.6 * math.exp(-0.3 * i)
            x2d, hn = _diff_layer(x2d, hn, b_w_in, b_lambda[j], b_subln[j], b_w_out, j,
                                  g_next, tab_b, lam_init, bsz, seq, final, tag=str(j))
    return x2d.reshape(bsz, seq, D_MODEL)
```

```python
import functools
import math

import jax
import jax.numpy as jnp
from jax import lax
from jax.experimental import pallas as pl
from jax.experimental.pallas import tpu as pltpu

D_MODEL = 1024
DEPTH = 4
BLK = 128
ROPE_THETA = 500000.0
EPS = 1e-6
NEG = -1e30
LOG2E = 1.4426950408889634

DILATED_GROUPS = ((128, 1), (512, 4), (2048, 16))
A_HEAD_DIM = 128
A_HEADS = 8
A_ROT = A_HEAD_DIM // 4
B_HEADS = 8
B_HEAD_DIM = 64
B_ROT = B_HEAD_DIM // 4

LANES = 128
COL_TILE = 1024
ROW_TILE = 512
PROJ_TILE = 1024
A_TQ = 1024
B_TQ = 1024
B_TK = 512
VMEM_LIMIT = 56 * 1024 * 1024

F32 = jnp.float32
BF16 = jnp.bfloat16


def _params(sem):
    return pltpu.CompilerParams(dimension_semantics=sem, vmem_limit_bytes=VMEM_LIMIT)


def _rope_tables(seq, head_dim, rot_dim):
    half = rot_dim // 2
    inv = 1.0 / (ROPE_THETA ** (jnp.arange(0, rot_dim, 2, dtype=F32) / rot_dim))
    ang = jnp.arange(seq, dtype=F32)[:, None] * inv[None, :]
    cos, sin = jnp.cos(ang), jnp.sin(ang)
    zeros = jnp.zeros((seq, head_dim - 2 * half), F32)
    zh = jnp.zeros((seq, half), F32)
    c = jnp.concatenate([cos, cos, jnp.ones_like(zeros)], axis=1)
    s1 = jnp.concatenate([zh, sin, zeros], axis=1)
    s2 = jnp.concatenate([-sin, zh, zeros], axis=1)
    reps = LANES // head_dim
    return tuple(jnp.tile(t, (1, reps)) for t in (c, s1, s2))


def _rms(x, g):
    ms = jnp.mean(x * x, axis=-1, keepdims=True)
    return x * lax.rsqrt(ms + EPS) * g


def _norm_kernel(x_ref, g_ref, o_ref):
    o_ref[...] = _rms(x_ref[...], g_ref[...]).astype(o_ref.dtype)


def _norm(x2d, g):
    rows = x2d.shape[0]
    return pl.pallas_call(
        _norm_kernel,
        grid=(rows // ROW_TILE,),
        in_specs=[pl.BlockSpec((ROW_TILE, D_MODEL), lambda i: (i, 0)),
                  pl.BlockSpec((1, D_MODEL), lambda i: (0, 0))],
        out_specs=pl.BlockSpec((ROW_TILE, D_MODEL), lambda i: (i, 0)),
        out_shape=jax.ShapeDtypeStruct((rows, D_MODEL), BF16),
        compiler_params=_params(("parallel",)),
        name="rmsnorm",
    )(x2d, g.reshape(1, D_MODEL))


def _strided_rows(ref, lead, p, rows, r):
    idx = pl.ds(p, rows, stride=r) if r > 1 else slice(None)
    return ref[lead + (idx, slice(None))]


_MODES = ("rope", "plain", "silu")


def _store_side(r):
    return r % 16 == 0


def _proj_kernel(tb_ref, hn_ref, w_ref, c_ref, s1_ref, s2_ref, o_ref, acc0, acc1, *,
                 r, shift, qscale, head_major):
    t = pl.program_id(0)
    tm = hn_ref.shape[0]
    rows = tm // r
    nslab = COL_TILE // LANES

    @pl.when(t == 0)
    def _():
        acc1[...] = jnp.zeros(acc1.shape, F32)

    section = tb_ref[3, t]
    mode_id = jnp.maximum(section - 1, 0)
    scale = jnp.where(section == 0, qscale, 1.0).astype(F32)

    pitch = rows + 8

    def matmul(acc_new):
        acc = jnp.dot(hn_ref[...], w_ref[...].astype(BF16), preferred_element_type=F32)
        for h in range(nslab):
            slab = acc[:, h * LANES:(h + 1) * LANES]
            if _store_side(r):
                for l in range(rows):
                    acc_new[h, pl.ds(l, r, stride=pitch), :] = slab[l * r:(l + 1) * r]
            else:
                acc_new[h] = slab

    def epilogue(acc_old, mode):
        for p in range(r):
            if mode == "rope":
                c = _strided_rows(c_ref, (), p, rows, r) * scale
                s1 = _strided_rows(s1_ref, (), p, rows, r) * scale
                s2 = _strided_rows(s2_ref, (), p, rows, r) * scale
            for h in range(nslab):
                if _store_side(r):
                    a = acc_old[h, p * pitch:p * pitch + rows, :]
                else:
                    a = _strided_rows(acc_old, (h,), p, rows, r)
                if mode == "rope":
                    a = (a * c + pltpu.roll(a, shift, 1) * s1
                         + pltpu.roll(a, LANES - shift, 1) * s2)
                elif mode == "silu":
                    a = a * jax.nn.sigmoid(a)
                if head_major:
                    o_ref[h] = a.astype(o_ref.dtype)
                else:
                    o_ref[p, :, h * LANES:(h + 1) * LANES] = a.astype(o_ref.dtype)

    def run(acc_new, acc_old, mode):
        if mode == "silu":
            epilogue(acc_old, mode)
            matmul(acc_new)
        else:
            matmul(acc_new)
            epilogue(acc_old, mode)

    for parity, (acc_new, acc_old) in enumerate(((acc0, acc1), (acc1, acc0))):
        for m, mode in enumerate(_MODES):
            @pl.when((t % 2 == parity) & (mode_id == m))
            def _(acc_new=acc_new, acc_old=acc_old, mode=mode):
                run(acc_new, acc_old, mode)


def _proj(hn, w_all, layer, tables, *, dilation, col_blocks, kinds, shift, qscale, name,
          head_major=False):
    assert not (head_major and dilation != 1)
    assert kinds[:3] == ("q", "k", "v") and kinds[3:] in ((), ("z",))
    bsz, seq, _ = hn.shape
    r = dilation
    tm = PROJ_TILE
    ni = seq // tm
    nk = len(kinds)
    n_tiles = nk * bsz * ni

    tiles = [(j, b, i) for j in range(nk) for b in range(bsz) for i in range(ni)]
    cur = tiles + [tiles[-1]]
    prev = [tiles[0]] + tiles
    tb = jnp.asarray([[b for _, b, _ in cur], [i for _, _, i in cur],
                      [col_blocks[j] for j, _, _ in cur], [j for j, _, _ in prev],
                      [b for _, b, _ in prev], [i for _, _, i in prev]], jnp.int32)

    nslab = COL_TILE // LANES
    acc_rows = r * (tm // r + 8) if _store_side(r) else tm
    if head_major:
        out_block, out_dims = (None, None, nslab, tm, LANES), (nk, bsz, nslab, seq, LANES)
    else:
        out_block, out_dims = (None, None, r, tm // r, COL_TILE), (nk, bsz, r, seq // r, COL_TILE)
    tab_spec = pl.BlockSpec((tm, LANES), lambda t, tb: (tb[5, t], 0))
    grid_spec = pltpu.PrefetchScalarGridSpec(
        num_scalar_prefetch=1,
        grid=(n_tiles + 1,),
        in_specs=[
            pl.BlockSpec((None, tm, D_MODEL), lambda t, tb: (tb[0, t], tb[1, t], 0)),
            pl.BlockSpec((None, D_MODEL, COL_TILE), lambda t, tb: (layer, 0, tb[2, t])),
            tab_spec, tab_spec, tab_spec,
        ],
        out_specs=pl.BlockSpec(out_block, lambda t, tb: (tb[3, t], tb[4, t], 0, tb[5, t], 0)),
        scratch_shapes=[pltpu.VMEM((nslab, acc_rows, LANES), F32),
                        pltpu.VMEM((nslab, acc_rows, LANES), F32)],
    )
    kern = functools.partial(_proj_kernel, r=r, shift=shift, qscale=qscale,
                             head_major=head_major)
    return pl.pallas_call(
        kern,
        grid_spec=grid_spec,
        out_shape=jax.ShapeDtypeStruct(out_dims, BF16),
        compiler_params=_params(("arbitrary",)),
        name=name,
    )(tb, hn, w_all, *tables)


def _dil_attn_kernel(q_ref, kc_ref, vc_ref, kp_ref, vp_ref, o_ref, ml_ref, kf, vf):
    n = pl.program_id(2)
    kf[0:BLK] = kp_ref[...]
    kf[BLK:] = kc_ref[...]
    tq = q_ref.shape[0]
    ones = jnp.ones((tq + BLK, A_HEAD_DIM), vf.dtype)
    for h in range(A_HEADS):
        sl = slice(h * A_HEAD_DIM, (h + 1) * A_HEAD_DIM)
        vf[0:BLK, 2 * h * A_HEAD_DIM:(2 * h + 1) * A_HEAD_DIM] = vp_ref[:, sl]
        vf[BLK:, 2 * h * A_HEAD_DIM:(2 * h + 1) * A_HEAD_DIM] = vc_ref[:, sl]
        vf[:, (2 * h + 1) * A_HEAD_DIM:(2 * h + 2) * A_HEAD_DIM] = ones

    row = lax.broadcasted_iota(jnp.int32, (BLK, 2 * BLK), 0)
    col = lax.broadcasted_iota(jnp.int32, (BLK, 2 * BLK), 1)
    band = (col >= row) & (col <= row + BLK)
    band_first = band & ((col >= BLK) | (n > 0))
    lane = lax.broadcasted_iota(jnp.int32, (BLK, LANES), 1)

    for i in range(tq // BLK):
        rows = slice(i * BLK, (i + 1) * BLK)
        keys = slice(i * BLK, (i + 2) * BLK)
        mask = band_first if i == 0 else band
        ml_tile = jnp.ones((BLK, LANES), F32)
        for h in range(A_HEADS):
            sl = slice(h * A_HEAD_DIM, (h + 1) * A_HEAD_DIM)
            q = q_ref[rows, sl]
            k = kf[keys, sl]
            v = vf[keys, 2 * h * A_HEAD_DIM:(2 * h + 2) * A_HEAD_DIM]
            s = lax.dot_general(q, k, (((1,), (1,)), ((), ())), preferred_element_type=F32)
            s = jnp.where(mask, s, NEG)
            m = jnp.max(s, axis=-1, keepdims=True)
            p = jnp.exp2(s - m)
            ol = jnp.dot(p.astype(BF16), v, preferred_element_type=F32)
            o_ref[rows, sl] = ol[:, :A_HEAD_DIM].astype(o_ref.dtype)
            ml_tile = jnp.where(lane == h, m, ml_tile)
            ml_tile = jnp.where(lane == A_HEADS + h, ol[:, A_HEAD_DIM:], ml_tile)
        ml_ref[rows, :] = ml_tile


def _dil_attn(qkv, *, name):
    _, bsz, r, length, _ = qkv.shape
    width = A_HEADS * A_HEAD_DIM
    tq = min(A_TQ, length)
    sub = tq // BLK
    blk_big = lambda c: pl.BlockSpec((None, None, None, tq, width),
                                     lambda b, p, n: (c, b, p, n, 0))
    blk_prev = lambda c: pl.BlockSpec(
        (None, None, None, BLK, width),
        lambda b, p, n: (c, b, p, jnp.maximum(n * sub - 1, 0), 0))
    return pl.pallas_call(
        _dil_attn_kernel,
        grid=(bsz, r, length // tq),
        in_specs=[blk_big(0), blk_big(1), blk_big(2), blk_prev(1), blk_prev(2)],
        out_specs=[pl.BlockSpec((None, None, tq, width), lambda b, p, n: (b, p, n, 0)),
                   pl.BlockSpec((None, None, tq, LANES), lambda b, p, n: (b, p, n, 0))],
        out_shape=[jax.ShapeDtypeStruct((bsz, r, length, width), BF16),
                   jax.ShapeDtypeStruct((bsz, r, length, LANES), F32)],
        scratch_shapes=[pltpu.VMEM((tq + BLK, width), BF16),
                        pltpu.VMEM((tq + BLK, 2 * width), BF16)],
        compiler_params=_params(("parallel", "parallel", "arbitrary")),
        name=name,
    )(qkv, qkv, qkv, qkv, qkv)


def _diff_attn_kernel(q_ref, k_ref, v_ref, lp_ref, g_ref, o_ref, qq, vx, tri, m_s, acc_s, s0, s1,
                      *, lam_init):
    qi = pl.program_id(2)
    hd = 2 * B_HEAD_DIM
    half = B_TQ // 2
    assert half == B_TK
    every = slice(0, 4 * half)
    top = slice(0, 2 * half)
    bottom = slice(2 * half, 4 * half)

    @pl.when(qi == 0)
    def _():
        vx[:, :hd] = v_ref[...]
        vx[:, hd:] = jnp.ones((vx.shape[0], hd), vx.dtype)
        r_i = lax.broadcasted_iota(jnp.int32, (half, B_TK), 0)
        c_i = lax.broadcasted_iota(jnp.int32, (half, B_TK), 1)
        tri[...] = jnp.where(c_i <= r_i, 0.0, NEG).astype(F32)

    lane = lax.broadcasted_iota(jnp.int32, (half, hd), 1)
    zero = jnp.zeros((half, hd), qq.dtype)
    for part in range(2):
        q = q_ref[part * half:(part + 1) * half, :]
        qq[(2 * part) * half:(2 * part + 1) * half] = jnp.where(lane < B_HEAD_DIM, q, zero)
        qq[(2 * part + 1) * half:(2 * part + 2) * half] = jnp.where(lane >= B_HEAD_DIM, q, zero)
    m_s[...] = jnp.full(m_s.shape, NEG, F32)
    acc_s[...] = jnp.zeros(acc_s.shape, F32)

    def scores(kb, buf, rows=every):
        start = pl.multiple_of(kb * B_TK, B_TK)
        k = k_ref[pl.ds(start, B_TK), :]
        buf[rows] = lax.dot_general(qq[rows], k, (((1,), (1,)), ((), ())),
                                    preferred_element_type=F32)

    def consume(kb, buf, rows=every, masked=None):
        start = pl.multiple_of(kb * B_TK, B_TK)
        v = vx[pl.ds(start, B_TK), :]
        if masked is None:
            s = buf[rows]
        else:
            bias = tri[...]
            parts = [buf[masked.start:masked.start + half] + bias,
                     buf[masked.start + half:masked.stop] + bias]
            if masked.stop < rows.stop:
                parts.append(buf[masked.stop:rows.stop])
            s = jnp.concatenate(parts, axis=0)
        m_prev = m_s[rows]
        m_new = jnp.maximum(m_prev, jnp.max(s, axis=-1, keepdims=True))
        alpha = jnp.exp2(m_prev - m_new)
        p = jnp.exp2(s - jnp.tile(m_new, (1, B_TK // LANES)))
        pv = jnp.dot(p.astype(BF16), v, preferred_element_type=F32)
        acc_s[rows] = jnp.tile(alpha, (1, 2)) * acc_s[rows] + pv
        m_s[rows] = m_new

    for c0 in (0, B_TK // 2):
        s0[:, c0:c0 + B_TK // 2] = lax.dot_general(
            qq[...], k_ref[c0:c0 + B_TK // 2, :], (((1,), (1,)), ((), ())),
            preferred_element_type=F32)

    def pair(b0):
        scores(b0 + 1, s1)
        consume(b0, s0)
        scores(b0 + 2, s0)
        consume(b0 + 1, s1)

    def quad(j, c):
        pair(4 * j)
        pair(4 * j + 2)
        return c

    lax.fori_loop(0, qi // 2, quad, 0)

    @pl.when(qi % 2 == 1)
    def _():
        pair(2 * qi - 2)

    scores(2 * qi + 1, s1, bottom)
    consume(2 * qi, s0, every, masked=top)
    consume(2 * qi + 1, s1, bottom, masked=bottom)

    lp = lp_ref[...]
    lam = (jnp.exp(jnp.sum(lp[0:1] * lp[1:2], axis=-1, keepdims=True))
           - jnp.exp(jnp.sum(lp[2:3] * lp[3:4], axis=-1, keepdims=True)) + lam_init)
    for part in range(2):
        lo = 2 * part * half
        on = acc_s[lo:lo + 2 * half, :hd] / acc_s[lo:lo + 2 * half, hd:]
        o = on[:half] - lam * on[half:]
        o = _rms(o, g_ref[...]) * (1.0 - lam_init)
        o_ref[part * half:(part + 1) * half, :] = o.astype(o_ref.dtype)


def _diff_attn(qkvz, lam_params, subln, lam_init, *, name):
    _, bsz, _, seq, _ = qkvz.shape
    hd = 2 * B_HEAD_DIM
    return pl.pallas_call(
        functools.partial(_diff_attn_kernel, lam_init=lam_init),
        grid=(bsz, B_HEADS, seq // B_TQ),
        in_specs=[pl.BlockSpec((None, None, None, B_TQ, hd), lambda b, h, i: (0, b, h, i, 0)),
                  pl.BlockSpec((None, None, None, seq, hd), lambda b, h, i: (1, b, h, 0, 0)),
                  pl.BlockSpec((None, None, None, seq, hd), lambda b, h, i: (2, b, h, 0, 0)),
                  pl.BlockSpec((4, B_HEAD_DIM), lambda b, h, i: (0, 0)),
                  pl.BlockSpec((1, hd), lambda b, h, i: (0, 0))],
        out_specs=pl.BlockSpec((None, None, B_TQ, hd), lambda b, h, i: (b, h, i, 0)),
        out_shape=jax.ShapeDtypeStruct((bsz, B_HEADS, seq, hd), BF16),
        scratch_shapes=[pltpu.VMEM((2 * B_TQ, hd), BF16),
                        pltpu.VMEM((seq, 2 * hd), BF16),
                        pltpu.VMEM((B_TK, B_TK), F32),
                        pltpu.VMEM((2 * B_TQ, LANES), F32),
                        pltpu.VMEM((2 * B_TQ, 2 * hd), F32),
                        pltpu.VMEM((2 * B_TQ, B_TK), F32),
                        pltpu.VMEM((2 * B_TQ, B_TK), F32)],
        compiler_params=_params(("arbitrary", "arbitrary", "arbitrary")),
        name=name,
    )(qkvz, qkvz, qkvz, lam_params, subln.reshape(1, hd))


def _finish(y, x_ref, w_ref, g_ref, xo_ref, hn_ref, final):
    xn = x_ref[...] + jnp.dot(y, w_ref[...], preferred_element_type=F32)
    if final:
        xo_ref[...] = _rms(xn, g_ref[...])
    else:
        xo_ref[...] = xn
        hn_ref[...] = _rms(xn, g_ref[...]).astype(hn_ref.dtype)


def _a_out_kernel(o1, o2, o3, l1, l2, l3, z_ref, x_ref, w_ref, g_ref, xo_ref, hn_ref,
                  t2, t3, lt2, lt3):
    for o_ref, l_ref, tok, lt in ((o2, l2, t2, lt2), (o3, l3, t3, lt3)):
        r, rows = o_ref.shape[0], o_ref.shape[1]
        for p in range(r):
            idx = pl.ds(p, rows, stride=r)
            lt[idx, :] = l_ref[p]
            for h in range(A_HEADS):
                tok[h, idx, :] = o_ref[p, :, h * LANES:(h + 1) * LANES].astype(F32)
    head_lane = lax.broadcasted_iota(jnp.int32, (ROW_TILE, LANES), 1) < A_HEADS
    ms = (l1[0], lt2[...], lt3[...])
    ls = [jnp.where(head_lane, pltpu.roll(t, LANES - A_HEADS, 1), 1.0) for t in ms]
    lses = [m_g + jnp.log2(l_g) for m_g, l_g in zip(ms, ls)]
    top = jnp.maximum(jnp.maximum(lses[0], lses[1]), lses[2])
    inv = 1.0 / (jnp.exp2(lses[0] - top) + jnp.exp2(lses[1] - top) + jnp.exp2(lses[2] - top))
    width = A_HEADS * A_HEAD_DIM
    spread = (lax.broadcasted_iota(jnp.int32, (LANES, width), 0)
              == lax.broadcasted_iota(jnp.int32, (LANES, width), 1) // A_HEAD_DIM).astype(BF16)
    w1, w2, w3 = (
        jnp.dot(jnp.where(head_lane, jnp.exp2(m_g - top) * inv, 0.0).astype(BF16), spread,
                preferred_element_type=F32)
        for m_g in ms)
    pieces = []
    for h in range(A_HEADS):
        sl = slice(h * A_HEAD_DIM, (h + 1) * A_HEAD_DIM)
        o = w1[:, sl] * o1[0, :, sl].astype(F32) + w2[:, sl] * t2[h] + w3[:, sl] * t3[h]
        pieces.append((o * z_ref[:, sl].astype(F32)).astype(BF16))
    y = jnp.concatenate(pieces, axis=1)
    _finish(y, x_ref, w_ref, g_ref, xo_ref, hn_ref, False)


def _b_out_kernel(o_ref, z_ref, x_ref, w_ref, g_ref, xo_ref, *maybe_hn, final):
    y = jnp.concatenate([(o_ref[h].astype(F32) * z_ref[h].astype(F32)).astype(BF16)
                         for h in range(B_HEADS)], axis=1)
    _finish(y, x_ref, w_ref, g_ref, xo_ref, None if final else maybe_hn[0], final)


def _row_spec(width, col=0):
    return pl.BlockSpec((ROW_TILE, width), lambda i: (i, col))


def _dilated_spec(arr, seq):
    _, r, _, width = arr.shape
    ni = seq // ROW_TILE
    return pl.BlockSpec((None, r, ROW_TILE // r, width), lambda i: (i // ni, 0, i % ni, 0))


def _out_call(kern, acts, act_specs, x2d, w_all, layer, g, *, final, name, scratch=()):
    rows = x2d.shape[0]
    out_shape = [jax.ShapeDtypeStruct((rows, D_MODEL), F32)]
    out_specs = [_row_spec(D_MODEL)]
    if not final:
        out_shape.append(jax.ShapeDtypeStruct((rows, D_MODEL), BF16))
        out_specs.append(_row_spec(D_MODEL))
    res = pl.pallas_call(
        kern,
        grid=(rows // ROW_TILE,),
        in_specs=act_specs + [_row_spec(D_MODEL),
                              pl.BlockSpec((None, D_MODEL, D_MODEL), lambda i: (layer, 0, 0)),
                              pl.BlockSpec((1, D_MODEL), lambda i: (0, 0))],
        out_specs=out_specs,
        out_shape=out_shape,
        scratch_shapes=list(scratch),
        compiler_params=_params(("parallel",)),
        name=name,
    )(*acts, x2d, w_all, g.reshape(1, D_MODEL))
    return res if not final else (res[0], None)


def _dilated_layer(x2d, hn, w_in, w_out, layer, g_next, tables, bsz, seq, tag):
    outs, lses = [], []
    gate = None
    qscale = (A_HEAD_DIM ** -0.5) * LOG2E
    for g, (window, r) in enumerate(DILATED_GROUPS):
        assert window // r == BLK
        with_gate = g == 0
        kinds = ("q", "k", "v", "z") if with_gate else ("q", "k", "v")
        cols = [3 * g, 3 * g + 1, 3 * g + 2] + ([9] if with_gate else [])
        qkv = _proj(hn.reshape(bsz, seq, D_MODEL), w_in, layer, tables, dilation=r,
                    col_blocks=cols, kinds=kinds, shift=A_ROT // 2, qscale=qscale,
                    name=f"a_proj{g}_{tag}")
        if with_gate:
            gate = qkv
        o, lse = _dil_attn(qkv, name=f"a_attn{g}_{tag}")
        outs.append(o)
        lses.append(lse)
    ni = seq // ROW_TILE
    gate_spec = pl.BlockSpec((None, None, None, ROW_TILE, COL_TILE),
                             lambda i: (3, i // ni, 0, i % ni, 0))
    acts = outs + lses + [gate]
    specs = [_dilated_spec(a, seq) for a in outs + lses] + [gate_spec]
    slabs = pltpu.VMEM((A_HEADS, ROW_TILE, LANES), F32)
    flat = pltpu.VMEM((ROW_TILE, LANES), F32)
    return _out_call(_a_out_kernel, acts, specs, x2d, w_out, layer, g_next, final=False,
                     name=f"a_out_{tag}", scratch=(slabs, slabs, flat, flat))


def _diff_layer(x2d, hn, w_in, lam_params, subln, w_out, layer, g_next, tables, lam_init,
                bsz, seq, final, tag):
    qscale = (B_HEAD_DIM ** -0.5) * LOG2E
    qkvz = _proj(hn.reshape(bsz, seq, D_MODEL), w_in, layer, tables, dilation=1,
                 col_blocks=[0, 1, 2, 3], kinds=("q", "k", "v", "z"), shift=B_ROT // 2,
                 qscale=qscale, name=f"b_proj_{tag}", head_major=True)
    o = _diff_attn(qkvz, lam_params, subln, lam_init, name=f"b_attn_{tag}")
    ni = seq // ROW_TILE
    hd = 2 * B_HEAD_DIM
    acts = [o, qkvz]
    specs = [pl.BlockSpec((None, B_HEADS, ROW_TILE, hd), lambda i: (i // ni, 0, i % ni, 0)),
             pl.BlockSpec((None, None, B_HEADS, ROW_TILE, hd),
                          lambda i: (3, i // ni, 0, i % ni, 0))]
    return _out_call(functools.partial(_b_out_kernel, final=final), acts, specs, x2d, w_out,
                     layer, g_next, final=final, name=f"b_out_{tag}")


def kernel(x, a_norm, a_w_in, a_w_out, b_norm, b_w_in, b_lambda, b_subln, b_w_out, final_norm):
    bsz, seq, _ = x.shape
    tab_a = _rope_tables(seq, A_HEAD_DIM, A_ROT)
    tab_b = _rope_tables(seq, B_HEAD_DIM, B_ROT)
    a_w_out, b_w_out = a_w_out.astype(BF16), b_w_out.astype(BF16)

    x2d = x.reshape(bsz * seq, D_MODEL)
    hn = _norm(x2d, a_norm[0])
    for i in range(DEPTH):
        j = i // 2
        if i % 2 == 0:
            x2d, hn = _dilated_layer(x2d, hn, a_w_in, a_w_out, j, b_norm[j], tab_a,
                                     bsz, seq, tag=str(j))
        else:
            final = i == DEPTH - 1
            g_next = final_norm if final else a_norm[j + 1]
            lam_init = 0.8 - 0.6 * math.exp(-0.3 * i)
            x2d, hn = _diff_layer(x2d, hn, b_w_in, b_lambda[j], b_subln[j], b_w_out, j,
                                  g_next, tab_b, lam_init, bsz, seq, final, tag=str(j))
    return x2d.reshape(bsz, seq, D_MODEL)
```
